```python
import math
import jax, jax.numpy as jnp
from jax import lax
import numpy as np

D_MODEL = 1024
BATCH = 4
SEQ = 4096
DEPTH = 1
DEC_BATCH = 128
DEC_SEQ = 8
PAST_LEN = 2048
PAGE_SIZE = 128

HEAD_DIM = 64
SB_HEADS = 8
SB_WIDTH = SB_HEADS * HEAD_DIM
RW_HEADS = 8
RW_WIDTH = RW_HEADS * HEAD_DIM
D_MIX = SB_WIDTH + RW_WIDTH
LORA_W = 64
LORA_A = 64
SB_COLS = 4 * SB_WIDTH
RW_COLS = 4 * RW_WIDTH + LORA_W + LORA_A
IN_COLS = SB_COLS + RW_COLS
Q_BLOCK = 128
RMS_EPS = 1e-6
GN_EPS = 64e-5
DECAY_OFFSET = 0.5
L2_EPS = 1e-12
SB_BIAS_INIT = -8.0

kernel_name = 'hymba_stickbreak_rwkv7_step'


def _rmsnorm(x, g):
    xf = x.astype(jnp.float32)
    y = xf * lax.rsqrt(jnp.mean(xf * xf, axis=-1, keepdims=True) + RMS_EPS)
    return (y * g.astype(jnp.float32)).astype(x.dtype)


def _project(h, norm_in, w_in):
    B, T, _ = h.shape
    p = _rmsnorm(h, norm_in) @ w_in
    q, k, v, g = jnp.split(p[..., :SB_COLS], 4, axis=-1)
    hd = lambda t: t.reshape(B, T, SB_HEADS, HEAD_DIM)
    return hd(q), hd(k), hd(v), g, p[..., SB_COLS:]


def _sb_attend(q, k, v, q_pos, k_pos, sb_bias):
    z = jnp.einsum('bqhd,bkhd->bhqk', q.astype(jnp.float32), k.astype(jnp.float32)) * (HEAD_DIM ** -0.5)
    z = z + sb_bias.astype(jnp.float32)[None, :, None, None]
    mask = (k_pos[None, :] < q_pos[:, None])[None, None]
    log_keep = jnp.where(mask, jax.nn.log_sigmoid(-z), 0.0)
    between = lax.cumsum(log_keep, axis=3, reverse=True) - log_keep
    a = jnp.where(mask, jnp.exp(jax.nn.log_sigmoid(z) + between), 0.0)
    return jnp.einsum('bhqk,bkhd->bqhd', a, v.astype(jnp.float32)).astype(q.dtype)


def _wkv_scan(r, w, k, v, a, b, s0):
    def step(S, inp):
        r_t, w_t, k_t, v_t, a_t, b_t = inp
        sa = jnp.einsum('bhvk,bhk->bhv', S, a_t)
        S = S * w_t[:, :, None, :] + sa[..., None] * b_t[:, :, None, :] + v_t[..., None] * k_t[:, :, None, :]
        return S, jnp.einsum('bhvk,bhk->bhv', S, r_t)
    xs = tuple(jnp.swapaxes(t.astype(jnp.float32), 0, 1) for t in (r, w, k, v, a, b))
    s_last, ys = lax.scan(step, s0.astype(jnp.float32), xs)
    return jnp.swapaxes(ys, 0, 1), s_last


def _rwkv_branch(p, prev, s0, mu, w0, w_up, a0, a_up, k_k, k_a, r_k, ln_w, ln_b):
    B, T, _ = p.shape
    shifted = jnp.concatenate([prev[:, None, :].astype(p.dtype), p[:, :-1]], axis=1)
    zc = p + (shifted - p) * mu
    r, k, v, g, wd, ad = jnp.split(zc, [RW_WIDTH, 2 * RW_WIDTH, 3 * RW_WIDTH, 4 * RW_WIDTH, 4 * RW_WIDTH + LORA_W], axis=-1)
    w_log = -jax.nn.softplus(-(w0 + jnp.tanh(wd) @ w_up)) - DECAY_OFFSET
    decay = jnp.exp(-jnp.exp(w_log.astype(jnp.float32)))
    alr = jax.nn.sigmoid((a0 + ad @ a_up).astype(jnp.float32))
    hd = lambda t: t.astype(jnp.float32).reshape(B, T, RW_HEADS, HEAD_DIM)
    kk = hd(k * k_k)
    kk = kk / jnp.maximum(jnp.sqrt(jnp.sum(kk * kk, axis=-1, keepdims=True)), L2_EPS)
    k_eff = k.astype(jnp.float32) * (1.0 + (alr - 1.0) * k_a.astype(jnp.float32))
    rh, kh, vh, ah = hd(r), hd(k_eff), hd(v), hd(alr)
    y, s_last = _wkv_scan(rh, hd(decay), kh, vh, -kk, kk * ah, s0)
    mean = jnp.mean(y, axis=-1, keepdims=True)
    var = jnp.mean(jnp.square(y - mean), axis=-1, keepdims=True)
    yn = ((y - mean) * lax.rsqrt(var + GN_EPS)).reshape(B, T, RW_WIDTH) * ln_w.astype(jnp.float32) + ln_b.astype(jnp.float32)
    bonus = (jnp.sum(rh * kh * r_k.astype(jnp.float32), axis=-1, keepdims=True) * vh).reshape(B, T, RW_WIDTH)
    out = (yn + bonus) * jax.nn.silu(g.astype(jnp.float32))
    return out.astype(p.dtype), s_last, p[:, -1]


def _layer(h_p, h_s, cache_k, cache_v, page_table, s_wkv, s_shift,
           norm_in, w_in, sb_bias, mu, w0, w_up, a0, a_up, k_k, k_a, r_k, ln_w, ln_b, w_out):
    rw = (mu, w0, w_up, a0, a_up, k_k, k_a, r_k, ln_w, ln_b)
    B, S, _ = h_p.shape
    q, k, v, g, p_rw = _project(h_p, norm_in, w_in)
    nb = S // Q_BLOCK
    pos = jnp.arange(S)
    q_blocks = jnp.swapaxes(q.reshape(B, nb, Q_BLOCK, SB_HEADS, HEAD_DIM), 0, 1)
    o = lax.map(lambda qp: _sb_attend(qp[0], k, v, qp[1], pos, sb_bias), (q_blocks, pos.reshape(nb, Q_BLOCK)))
    o_sb = jnp.swapaxes(o, 0, 1).reshape(B, S, SB_WIDTH) * jax.nn.silu(g)
    o_rw, wkv_p, shift_p = _rwkv_branch(
        p_rw, jnp.zeros((B, RW_COLS), p_rw.dtype),
        jnp.zeros((B, RW_HEADS, HEAD_DIM, HEAD_DIM), jnp.float32), *rw)
    h_p = h_p + jnp.concatenate([o_sb, o_rw], axis=-1) @ w_out
    Bd, T, _ = h_s.shape
    past = page_table.shape[1] * cache_k.shape[1]
    qs, ks, vs, gs, ps_rw = _project(h_s, norm_in, w_in)
    k_past = cache_k[page_table].reshape(Bd, past, SB_HEADS, HEAD_DIM).astype(ks.dtype)
    v_past = cache_v[page_table].reshape(Bd, past, SB_HEADS, HEAD_DIM).astype(vs.dtype)
    k_all = jnp.concatenate([k_past, ks], axis=1)
    v_all = jnp.concatenate([v_past, vs], axis=1)
    os_ = _sb_attend(qs, k_all, v_all, past + jnp.arange(T), jnp.arange(past + T), sb_bias)
    os_sb = os_.reshape(Bd, T, SB_WIDTH) * jax.nn.silu(gs)
    os_rw, wkv_s, shift_s = _rwkv_branch(ps_rw, s_shift, s_wkv, *rw)
    h_s = h_s + jnp.concatenate([os_sb, os_rw], axis=-1) @ w_out
    return (h_p, h_s, k, v, wkv_p.astype(s_wkv.dtype), shift_p.astype(s_shift.dtype),
            ks, vs, wkv_s.astype(s_wkv.dtype), shift_s.astype(s_shift.dtype))


def setup_inputs(seed: int = 0) -> dict:
    key = jax.random.key(seed)
    ks = jax.random.split(key, 24)
    f32 = jnp.float32
    n_pages = PAST_LEN // PAGE_SIZE
    n_used = DEC_BATCH * n_pages
    n_phys = n_used + max(1, n_used // 4)
    nrm = lambda i, shape, s: jax.random.normal(ks[i], shape, f32) * s
    page_table = jax.random.permutation(ks[5], n_phys)[:n_used].reshape(DEC_BATCH, n_pages).astype(jnp.int32)
    return {
        'x_prompt': nrm(0, (BATCH, SEQ, D_MODEL), 1.0),
        'x_sample': nrm(1, (DEC_BATCH, DEC_SEQ, D_MODEL), 1.0),
        'cache_k': nrm(2, (DEPTH, n_phys, PAGE_SIZE, SB_HEADS, HEAD_DIM), 1.0),
        'cache_v': nrm(3, (DEPTH, n_phys, PAGE_SIZE, SB_HEADS, HEAD_DIM), 1.0),
        'page_table': page_table,
        'state_wkv': nrm(4, (DEPTH, DEC_BATCH, RW_HEADS, HEAD_DIM, HEAD_DIM), 0.1),
        'state_shift': nrm(6, (DEPTH, DEC_BATCH, RW_COLS), 1.0),
        'norm_in': 1.0 + nrm(7, (DEPTH, D_MODEL), 0.02),
        'w_in': nrm(8, (DEPTH, D_MODEL, IN_COLS), D_MODEL ** -0.5),
        'sb_bias': SB_BIAS_INIT + nrm(21, (DEPTH, SB_HEADS), 0.1),
        'tshift_mu': jax.random.uniform(ks[9], (DEPTH, RW_COLS), f32),
        'w0': nrm(10, (DEPTH, RW_WIDTH), 0.5),
        'w_up': nrm(11, (DEPTH, LORA_W, RW_WIDTH), LORA_W ** -0.5),
        'a0': nrm(12, (DEPTH, RW_WIDTH), 0.5),
        'a_up': nrm(13, (DEPTH, LORA_A, RW_WIDTH), LORA_A ** -0.5),
        'k_k': 0.85 + nrm(14, (DEPTH, RW_WIDTH), 0.05),
        'k_a': 1.0 + nrm(15, (DEPTH, RW_WIDTH), 0.05),
        'r_k': nrm(16, (DEPTH, RW_HEADS, HEAD_DIM), 0.1),
        'ln_w': 1.0 + nrm(17, (DEPTH, RW_WIDTH), 0.02),
        'ln_b': nrm(18, (DEPTH, RW_WIDTH), 0.02),
        'w_out': nrm(19, (DEPTH, D_MIX, D_MODEL), D_MIX ** -0.5),
        'norm_f': 1.0 + nrm(20, (D_MODEL,), 0.02),
    }


def reference(x_prompt, x_sample, cache_k, cache_v, page_table, state_wkv, state_shift,
              norm_in, w_in, sb_bias, tshift_mu, w0, w_up, a0, a_up, k_k, k_a, r_k, ln_w, ln_b, w_out, norm_f):
    h_p, h_s = x_prompt, x_sample
    kp, vp, wp, sp, kd, vd, wd, sd = [], [], [], [], [], [], [], []
    for l in range(DEPTH):
        out = _layer(h_p, h_s, cache_k[l], cache_v[l], page_table, state_wkv[l], state_shift[l],
                     norm_in[l], w_in[l], sb_bias[l], tshift_mu[l], w0[l], w_up[l], a0[l], a_up[l],
                     k_k[l], k_a[l], r_k[l], ln_w[l], ln_b[l], w_out[l])
        h_p, h_s = out[0], out[1]
        for lst, t in zip((kp, vp, wp, sp, kd, vd, wd, sd), out[2:]):
            lst.append(t)
    y_prompt = _rmsnorm(h_p, norm_f)
    y_sample = _rmsnorm(h_s, norm_f)
    return (y_prompt, y_sample, jnp.stack(kp), jnp.stack(vp), jnp.stack(wp), jnp.stack(sp),
            jnp.stack(kd), jnp.stack(vd), jnp.stack(wd), jnp.stack(sd))
```

```python
import functools

import jax
import jax.numpy as jnp
from jax import lax
from jax.experimental import pallas as pl
from jax.experimental.pallas import tpu as pltpu

F32 = jnp.float32
BF16 = jnp.bfloat16

HEAD_DIM = 64
LANES = 128
LORA = 64
RMS_EPS = 1e-6
GN_EPS = 64e-5
DECAY_OFFSET = 0.5
L2_EPS = 1e-12
VMEM_LIMIT = 48 * 1024 * 1024

_NN = (((1,), (0,)), ((), ()))
_NT = (((1,), (1,)), ((), ()))
_TN = (((0,), (0,)), ((), ()))


def _mm(a, b, dims=_NN):
    return lax.dot_general(a, b, dims, preferred_element_type=F32)


def _split2(x):
    hi = x.astype(BF16)
    lo = (x - hi.astype(F32)).astype(BF16)
    return hi, lo


def _split3(x):
    hi = x.astype(BF16)
    r = x - hi.astype(F32)
    mid = r.astype(BF16)
    lo = (r - mid.astype(F32)).astype(BF16)
    return hi, mid, lo


def _dot3(a, b, dims=_NN):
    ah, al = _split2(a)
    bh, bl = _split2(b)
    return _mm(ah, bh, dims) + (_mm(ah, bl, dims) + _mm(al, bh, dims))


def _dot_x2(x, m, dims=_NN):
    hi, lo = _split2(x)
    return _mm(hi, m, dims) + _mm(lo, m, dims)


def _softplus(z):
    return jnp.maximum(z, 0.0) + jnp.log1p(jnp.exp(-jnp.abs(z)))


def _silu(g):
    return g * jax.nn.sigmoid(g)


def _inproj_kernel(x_ref, gn_ref, w_ref, k_ref, v_ref, sg_ref, rw_ref, q_ref, kb_ref, vb_ref, *, sbw):
    x = x_ref[...]
    xn = x * lax.rsqrt(jnp.mean(x * x, axis=-1, keepdims=True) + RMS_EPS) * gn_ref[...]
    xb = xn.astype(BF16)

    def proj(lo, hi):
        return _mm(xb, w_ref[:, lo:hi])

    q = proj(0, sbw)
    q_ref[...] = (q * HEAD_DIM ** -0.5).astype(q_ref.dtype)
    k = proj(sbw, 2 * sbw)
    k_ref[...] = k
    kb_ref[...] = k.astype(BF16)
    v = proj(2 * sbw, 3 * sbw)
    v_ref[...] = v
    vb_ref[...] = v.astype(BF16)
    sg_ref[...] = _silu(proj(3 * sbw, 4 * sbw))
    rw_ref[...] = proj(4 * sbw, w_ref.shape[1])


def _inproj(x2d, gn, w_bf, sbw, q_dtype, tm):
    m, d = x2d.shape
    tm = min(tm, m)
    ncols = w_bf.shape[1]
    rwc = ncols - 4 * sbw
    row = lambda i: (i, 0)
    fixed = lambda i: (0, 0)
    slab = lambda dt: jax.ShapeDtypeStruct((m, sbw), dt)
    return pl.pallas_call(
        functools.partial(_inproj_kernel, sbw=sbw),
        grid=(m // tm,),
        in_specs=[pl.BlockSpec((tm, d), row), pl.BlockSpec((1, d), fixed), pl.BlockSpec((d, ncols), fixed)],
        out_specs=[pl.BlockSpec((tm, sbw), row), pl.BlockSpec((tm, sbw), row), pl.BlockSpec((tm, sbw), row),
                   pl.BlockSpec((tm, rwc), row), pl.BlockSpec((tm, sbw), row), pl.BlockSpec((tm, sbw), row),
                   pl.BlockSpec((tm, sbw), row)],
        out_shape=[slab(F32), slab(F32), slab(F32), jax.ShapeDtypeStruct((m, rwc), F32),
                   slab(q_dtype), slab(BF16), slab(BF16)],
        compiler_params=pltpu.CompilerParams(dimension_semantics=("arbitrary",), vmem_limit_bytes=VMEM_LIMIT),
        name="inproj",
    )(x2d, gn, w_bf)


def _sb_tile(z, u, carry, mask):
    sp = _softplus(z)
    lk = -sp
    if mask is not None:
        lk = jnp.where(mask, lk, 0.0)
    incl = _dot_x2(lk, u)
    a = jnp.exp((z - sp) + (incl - lk) + carry)
    if mask is not None:
        a = jnp.where(mask, a, 0.0)
    return a, incl[:, 0:1]


def _sb_prompt_kernel(bias_ref, q_ref, k_ref, v_ref, sg_ref, u_ref, o_ref, acc_scr, carry_scr, *, tq, tk):
    hp = pl.program_id(1)
    i = pl.program_id(2)
    q2 = q_ref[0]
    in_h0 = lax.broadcasted_iota(jnp.int32, (tq, LANES), 1) < HEAD_DIM
    zero = jnp.zeros_like(q2)
    qs = (jnp.where(in_h0, q2, zero), jnp.where(in_h0, zero, q2))
    biases = (bias_ref[2 * hp], bias_ref[2 * hp + 1])
    u = u_ref[...]
    jd = lax.div(i * tq, tk)

    acc_scr[...] = jnp.zeros_like(acc_scr)
    carry_scr[...] = jnp.zeros_like(carry_scr)

    def tile(j, mask):
        start = pl.multiple_of(j * tk, tk)
        ks = k_ref[0, pl.ds(start, tk), :]
        vs = v_ref[0, pl.ds(start, tk), :]
        for h in range(2):
            z = _mm(qs[h], ks, _NT) + biases[h]
            a, tot = _sb_tile(z, u, carry_scr[h], mask)
            acc_scr[h] += _mm(a.astype(BF16), vs)
            carry_scr[h] += tot

    qpos = i * tq + lax.broadcasted_iota(jnp.int32, (tq, tk), 0)
    kpos = jd * tk + lax.broadcasted_iota(jnp.int32, (tq, tk), 1)
    tile(jd, kpos < qpos)

    def body(t, _):
        tile(jd - 1 - t, None)
        return 0

    lax.fori_loop(0, jd, body, 0)
    o = jnp.where(in_h0, acc_scr[0], acc_scr[1]) * sg_ref[0]
    o_ref[0] = o.astype(o_ref.dtype)


def _sb_prompt(qb, kb, vb, sg, bias, tq, tk):
    b, s, w = qb.shape
    hpairs = w // LANES
    ri = lax.broadcasted_iota(jnp.int32, (tk, tk), 0)
    ci = lax.broadcasted_iota(jnp.int32, (tk, tk), 1)
    u = (ri >= ci).astype(BF16)
    qmap = lambda bi, hp, i: (bi, i, hp)
    kvmap = lambda bi, hp, i: (bi, 0, hp)
    return pl.pallas_call(
        functools.partial(_sb_prompt_kernel, tq=tq, tk=tk),
        grid=(b, hpairs, s // tq),
        in_specs=[pl.BlockSpec(memory_space=pltpu.SMEM),
                  pl.BlockSpec((1, tq, LANES), qmap),
                  pl.BlockSpec((1, s, LANES), kvmap),
                  pl.BlockSpec((1, s, LANES), kvmap),
                  pl.BlockSpec((1, tq, LANES), qmap),
                  pl.BlockSpec((tk, tk), lambda bi, hp, i: (0, 0))],
        out_specs=pl.BlockSpec((1, tq, LANES), qmap),
        out_shape=jax.ShapeDtypeStruct((b, s, w), BF16),
        scratch_shapes=[pltpu.VMEM((2, tq, LANES), F32), pltpu.VMEM((2, tq, 1), F32)],
        compiler_params=pltpu.CompilerParams(dimension_semantics=("arbitrary",) * 3, vmem_limit_bytes=VMEM_LIMIT),
        name="sb_prompt",
    )(bias, qb, kb, vb, sg, u)


def _sb_sample_kernel(pt_ref, q_ref, kn_ref, vn_ref, sg_ref, brow_ref, u_ref, *rest, pp, t, h, page):
    page_refs = rest[:2 * pp]
    o_ref, acc_scr, carry_scr = rest[2 * pp:]
    del pt_ref
    j = pl.program_id(1)
    ht = h * t
    w = h * HEAD_DIM
    row = lax.broadcasted_iota(jnp.int32, (ht, w), 0)
    lane = lax.broadcasted_iota(jnp.int32, (ht, w), 1)
    own = (row // t) == (lane // HEAD_DIM)
    qt = jnp.concatenate([q_ref[0]] * h, axis=0)
    qbd = jnp.where(own, qt, 0.0).astype(BF16)
    brow = brow_ref[...]
    u = u_ref[...]

    def block(kb, vb, mask):
        z = _mm(qbd, kb, _NT) + brow
        a, tot = _sb_tile(z, u, carry_scr[...], mask)
        acc_scr[...] += _mm(a.astype(BF16), vb)
        carry_scr[...] += tot

    @pl.when(j == 0)
    def _():
        acc_scr[...] = jnp.zeros_like(acc_scr)
        carry_scr[...] = jnp.zeros_like(carry_scr)
        pad = jnp.zeros((page - t, w), F32)
        kn = jnp.concatenate([kn_ref[0], pad], axis=0).astype(BF16)
        vn = jnp.concatenate([vn_ref[0], pad], axis=0).astype(BF16)
        key = lax.broadcasted_iota(jnp.int32, (ht, page), 1)
        tok = lax.broadcasted_iota(jnp.int32, (ht, page), 0) % t
        block(kn, vn, key < tok)

    for s in range(pp):
        block(page_refs[2 * s][0].astype(BF16), page_refs[2 * s + 1][0].astype(BF16), None)

    @pl.when(j == pl.num_programs(1) - 1)
    def _():
        sel = jnp.where(own, acc_scr[...], 0.0)
        o = sel[0:t]
        for hh in range(1, h):
            o = o + sel[hh * t:(hh + 1) * t]
        o_ref[0] = o * sg_ref[0]


def _sb_sample(q, kn, vn, sg, cache_k, cache_v, page_table, bias, pp):
    bd, t, w = q.shape
    h = w // HEAD_DIM
    n_phys, page, _ = cache_k.shape
    n_pages = page_table.shape[1]
    brow = jnp.broadcast_to(jnp.repeat(bias, t)[:, None], (h * t, page)).astype(F32)
    ri = lax.broadcasted_iota(jnp.int32, (page, page), 0)
    ci = lax.broadcasted_iota(jnp.int32, (page, page), 1)
    u = (ri >= ci).astype(BF16)
    tok = lambda b, j, pt: (b, 0, 0)
    fixed = lambda b, j, pt: (0, 0)

    def page_map(s):
        return lambda b, j, pt: (pt[b, n_pages - 1 - (j * pp + s)], 0, 0)

    page_specs, page_args = [], []
    for s in range(pp):
        page_specs += [pl.BlockSpec((1, page, w), page_map(s)), pl.BlockSpec((1, page, w), page_map(s))]
        page_args += [cache_k, cache_v]
    grid_spec = pltpu.PrefetchScalarGridSpec(
        num_scalar_prefetch=1,
        grid=(bd, n_pages // pp),
        in_specs=[pl.BlockSpec((1, t, w), tok), pl.BlockSpec((1, t, w), tok), pl.BlockSpec((1, t, w), tok),
                  pl.BlockSpec((1, t, w), tok), pl.BlockSpec((h * t, page), fixed), pl.BlockSpec((page, page), fixed)]
        + page_specs,
        out_specs=pl.BlockSpec((1, t, w), tok),
        scratch_shapes=[pltpu.VMEM((h * t, w), F32), pltpu.VMEM((h * t, 1), F32)],
    )
    return pl.pallas_call(
        functools.partial(_sb_sample_kernel, pp=pp, t=t, h=h, page=page),
        grid_spec=grid_spec,
        out_shape=jax.ShapeDtypeStruct((bd, t, w), F32),
        compiler_params=pltpu.CompilerParams(dimension_semantics=("arbitrary",) * 2, vmem_limit_bytes=VMEM_LIMIT),
        name="sb_sample",
    )(page_table, q, kn, vn, sg, brow, u, *page_args)


def _rwkv_kernel(p_ref, prev_ref, s0_ref, mu_ref, w0_ref, a0_ref, lora_ref, kk_ref, ka_ref, rk_ref,
                 lnw_ref, lnb_ref, seg_ref, o_ref, s_ref, prev_scr, *, g, tc, h):
    c = pl.program_id(1)
    r_rows = g * tc
    w = h * HEAD_DIM

    @pl.when(c == 0)
    def _():
        s_ref[...] = s0_ref[...]
        prev_scr[...] = prev_ref[...]

    p = p_ref[...]
    rowi = lax.broadcasted_iota(jnp.int32, (r_rows, 1), 0)
    prev_full = jnp.concatenate([jnp.broadcast_to(prev_scr[gi], (tc, p.shape[1])) for gi in range(g)], axis=0)
    shifted = jnp.where(rowi % tc == 0, prev_full, pltpu.roll(p, 1, 0))
    for gi in range(g):
        prev_scr[gi] = p[(gi + 1) * tc - 1:(gi + 1) * tc, :]
    zc = p + (shifted - p) * mu_ref[...]

    r = zc[:, 0:w]
    k = zc[:, w:2 * w]
    v = zc[:, 2 * w:3 * w]
    gate = zc[:, 3 * w:4 * w]
    wa = zc[:, 4 * w:4 * w + 2 * LORA]
    lane = lax.broadcasted_iota(jnp.int32, wa.shape, 1)
    wa = jnp.where(lane < LORA, jnp.tanh(wa), wa)
    up = _dot3(wa, lora_ref[...])
    w_log = -_softplus(-(w0_ref[...] + up[:, 0:w])) - DECAY_OFFSET
    e = jnp.exp(w_log)
    alr = jax.nn.sigmoid(a0_ref[...] + up[:, w:2 * w])

    ri = lax.broadcasted_iota(jnp.int32, (r_rows, r_rows), 0)
    ci = lax.broadcasted_iota(jnp.int32, (r_rows, r_rows), 1)
    strict = ci < ri
    incl = ci <= ri
    if g > 1:
        same = (ri // tc) == (ci // tc)
        strict = strict & same
        incl = incl & same
    eye = (ri == ci).astype(F32)
    lmat = incl.astype(BF16)
    e_hi, e_mid, e_lo = _split3(e)
    cum = -(_mm(lmat, e_hi) + (_mm(lmat, e_mid) + _mm(lmat, e_lo)))
    p_in = jnp.exp(cum)
    p_ex = jnp.exp(cum + e)
    p_inv = jnp.exp(-cum)

    seg = seg_ref[...]
    kkr = k * kk_ref[...]
    kk = kkr / jnp.maximum(jnp.sqrt(_dot_x2(kkr * kkr, seg)), L2_EPS)
    keff = k * (1.0 + (alr - 1.0) * ka_ref[...])
    at = -kk * p_ex
    bt = kk * alr * p_inv
    kt = keff * p_inv
    rt = r * p_in
    bonus = _dot_x2(r * keff * rk_ref[...], seg)

    n_double = max(tc.bit_length() - 2, 0)
    ys = []
    for hh in range(h):
        sl = slice(hh * HEAD_DIM, (hh + 1) * HEAD_DIM)
        at_h, bt_h, kt_h, rt_h, v_h = at[:, sl], bt[:, sl], kt[:, sl], rt[:, sl], v[:, sl]
        ar = jnp.concatenate([at_h, rt_h], axis=0)
        x1 = _dot3(ar, bt_h, _NT)
        x2 = _dot3(ar, kt_h, _NT)
        a_ab = jnp.where(strict, x1[:r_rows], 0.0)
        a_rb = jnp.where(incl, x1[r_rows:], 0.0)
        a_ak = jnp.where(strict, x2[:r_rows], 0.0)
        a_rk = jnp.where(incl, x2[r_rows:], 0.0)
        pw = a_ab
        tinv = eye + a_ab
        for _ in range(n_double):
            pw = _dot3(pw, pw)
            tinv = tinv + _dot3(tinv, pw)
        w_m = _dot3(tinv, at_h)
        u0 = _dot3(tinv, _dot3(a_ak, v_h))
        us, yr = [], []
        for gi in range(g):
            rows = slice(gi * tc, (gi + 1) * tc)
            wr = jnp.concatenate([w_m[rows], rt_h[rows]], axis=0)
            x = _dot3(wr, s_ref[gi, hh], _NT)
            us.append(x[:tc] + u0[rows])
            yr.append(x[tc:])
        u_m = us[0] if g == 1 else jnp.concatenate(us, axis=0)
        y_r = yr[0] if g == 1 else jnp.concatenate(yr, axis=0)
        ys.append(y_r + _dot3(a_rb, u_m) + _dot3(a_rk, v_h))
        bk = jnp.concatenate([bt_h, kt_h], axis=0)
        for gi in range(g):
            if g == 1:
                uv = jnp.concatenate([u_m, v_h], axis=0)
            else:
                mine = (rowi // tc) == gi
                uv = jnp.concatenate([jnp.where(mine, u_m, 0.0), jnp.where(mine, v_h, 0.0)], axis=0)
            last = (gi + 1) * tc - 1
            s_ref[gi, hh] = (s_ref[gi, hh] + _dot3(uv, bk, _TN)) * p_in[last:last + 1, sl]

    y = jnp.concatenate(ys, axis=1)
    mean = _dot_x2(y, seg) * (1.0 / HEAD_DIM)
    d = y - mean
    var = _dot_x2(d * d, seg) * (1.0 / HEAD_DIM)
    yn = d * lax.rsqrt(var + GN_EPS) * lnw_ref[...] + lnb_ref[...]
    o_ref[...] = ((yn + bonus * v) * _silu(gate)).astype(o_ref.dtype)


def _rwkv(p2d, prev, s0, params, g, tc, out_dtype):
    b, h = s0.shape[0], s0.shape[1]
    cols = p2d.shape[1]
    t = p2d.shape[0] // b
    nc = t // tc
    w = h * HEAD_DIM
    rows = g * tc
    fixed = lambda bi, c: (0, 0)
    seq = lambda bi, c: (bi, 0, 0)
    st = lambda bi, c: (bi, 0, 0, 0)
    rowmap = lambda bi, c: (bi * nc + c, 0)
    vec = lambda n: pl.BlockSpec((1, n), fixed)
    return pl.pallas_call(
        functools.partial(_rwkv_kernel, g=g, tc=tc, h=h),
        grid=(b // g, nc),
        in_specs=[pl.BlockSpec((rows, cols), rowmap), pl.BlockSpec((g, 1, cols), seq),
                  pl.BlockSpec((g, h, HEAD_DIM, HEAD_DIM), st),
                  vec(cols), vec(w), vec(w), pl.BlockSpec((2 * LORA, 2 * w), fixed),
                  vec(w), vec(w), vec(w), vec(w), vec(w), pl.BlockSpec((w, w), fixed)],
        out_specs=[pl.BlockSpec((rows, w), rowmap), pl.BlockSpec((g, h, HEAD_DIM, HEAD_DIM), st)],
        out_shape=[jax.ShapeDtypeStruct((b * t, w), out_dtype), jax.ShapeDtypeStruct(s0.shape, F32)],
        scratch_shapes=[pltpu.VMEM((g, 1, cols), F32)],
        compiler_params=pltpu.CompilerParams(dimension_semantics=("arbitrary",) * 2, vmem_limit_bytes=VMEM_LIMIT),
        name="rwkv",
    )(p2d, prev, s0, *params)


def _outproj_kernel(osb_ref, orw_ref, x_ref, w_ref, nf_ref, y_ref, *, sbw):
    hid = x_ref[...] + _mm(osb_ref[...].astype(BF16), w_ref[0:sbw, :]) + _mm(orw_ref[...].astype(BF16), w_ref[sbw:, :])
    y_ref[...] = hid * lax.rsqrt(jnp.mean(hid * hid, axis=-1, keepdims=True) + RMS_EPS) * nf_ref[...]


def _outproj(osb, orw, x2d, w_bf, nf, tm):
    m, d = x2d.shape
    tm = min(tm, m)
    sbw = osb.shape[1]
    row = lambda i: (i, 0)
    fixed = lambda i: (0, 0)
    return pl.pallas_call(
        functools.partial(_outproj_kernel, sbw=sbw),
        grid=(m // tm,),
        in_specs=[pl.BlockSpec((tm, sbw), row), pl.BlockSpec((tm, orw.shape[1]), row), pl.BlockSpec((tm, d), row),
                  pl.BlockSpec(w_bf.shape, fixed), pl.BlockSpec((1, d), fixed)],
        out_specs=pl.BlockSpec((tm, d), row),
        out_shape=jax.ShapeDtypeStruct((m, d), F32),
        compiler_params=pltpu.CompilerParams(dimension_semantics=("arbitrary",), vmem_limit_bytes=VMEM_LIMIT),
        name="outproj",
    )(osb, orw, x2d, w_bf, nf)


def _layer(h_p, h_s, cache_k, cache_v, page_table, s_wkv, s_shift,
           norm_in, w_in, sb_bias, mu, w0, w_up, a0, a_up, k_k, k_a, r_k, ln_w, ln_b, w_out, norm_out):
    b, s, d = h_p.shape
    bd, t, _ = h_s.shape
    n_heads = sb_bias.shape[0]
    sbw = n_heads * HEAD_DIM
    rw_heads = r_k.shape[0]
    rww = rw_heads * HEAD_DIM
    cols = mu.shape[0]

    w_in_bf = w_in.astype(BF16)
    w_out_bf = w_out.astype(BF16)
    gn = norm_in.reshape(1, d)
    nf = norm_out.reshape(1, d)
    zeros = jnp.zeros((LORA, rww), F32)
    lora = jnp.concatenate([jnp.concatenate([w_up, zeros], axis=1), jnp.concatenate([zeros, a_up], axis=1)], axis=0)
    hi = lax.broadcasted_iota(jnp.int32, (rww, rww), 0) // HEAD_DIM
    hj = lax.broadcasted_iota(jnp.int32, (rww, rww), 1) // HEAD_DIM
    seg = (hi == hj).astype(BF16)
    rw_params = (mu.reshape(1, cols), w0.reshape(1, rww), a0.reshape(1, rww), lora, k_k.reshape(1, rww),
                 k_a.reshape(1, rww), r_k.reshape(1, rww), ln_w.reshape(1, rww), ln_b.reshape(1, rww), seg)

    xp = h_p.reshape(b * s, d)
    k_p, v_p, sg_p, rw_p, qb_p, kb_p, vb_p = _inproj(xp, gn, w_in_bf, sbw, BF16, 256)
    three = lambda a: a.reshape(b, s, sbw)
    o_sb = _sb_prompt(three(qb_p), three(kb_p), three(vb_p), three(sg_p), sb_bias, 128, 256)
    o_rw, wkv_p = _rwkv(rw_p, jnp.zeros((b, 1, cols), F32), jnp.zeros((b, rw_heads, HEAD_DIM, HEAD_DIM), F32),
                        rw_params, 1, 64, BF16)
    y_p = _outproj(o_sb.reshape(b * s, sbw), o_rw, xp, w_out_bf, nf, 512)
    shift_p = rw_p.reshape(b, s, cols)[:, s - 1]

    xs = h_s.reshape(bd * t, d)
    k_s, v_s, sg_s, rw_s, q_s, _, _ = _inproj(xs, gn, w_in_bf, sbw, F32, 256)
    three = lambda a: a.reshape(bd, t, sbw)
    n_phys, page = cache_k.shape[0], cache_k.shape[1]
    os_sb = _sb_sample(three(q_s), three(k_s), three(v_s), three(sg_s), cache_k.reshape(n_phys, page, sbw),
                       cache_v.reshape(n_phys, page, sbw), page_table, sb_bias, 4)
    os_rw, wkv_s = _rwkv(rw_s, s_shift.reshape(bd, 1, cols), s_wkv, rw_params, 8, t, F32)
    y_s = _outproj(os_sb.reshape(bd * t, sbw), os_rw, xs, w_out_bf, nf, 512)
    shift_s = rw_s.reshape(bd, t, cols)[:, t - 1]

    heads = lambda a, n, tt: a.reshape(n, tt, n_heads, HEAD_DIM)
    return (y_p.reshape(b, s, d), y_s.reshape(bd, t, d), heads(k_p, b, s), heads(v_p, b, s), wkv_p, shift_p,
            heads(k_s, bd, t), heads(v_s, bd, t), wkv_s, shift_s)


def kernel(x_prompt, x_sample, cache_k, cache_v, page_table, state_wkv, state_shift, norm_in, w_in, sb_bias,
           tshift_mu, w0, w_up, a0, a_up, k_k, k_a, r_k, ln_w, ln_b, w_out, norm_f):
    assert norm_in.shape[0] == 1, "single-layer trunk"
    out = _layer(x_prompt, x_sample, cache_k[0], cache_v[0], page_table, state_wkv[0], state_shift[0],
                 norm_in[0], w_in[0], sb_bias[0], tshift_mu[0], w0[0], w_up[0], a0[0], a_up[0],
                 k_k[0], k_a[0], r_k[0], ln_w[0], ln_b[0], w_out[0], norm_f)
    y_p, y_s = out[0], out[1]
    return (y_p, y_s) + tuple(o[None] for o in out[2:])
```

```python
import functools

import jax
import jax.numpy as jnp
from jax import lax
from jax.experimental import pallas as pl
from jax.experimental.pallas import tpu as pltpu

F32 = jnp.float32
BF16 = jnp.bfloat16

HEAD_DIM = 64
LANES = 128
LORA = 64
RMS_EPS = 1e-6
GN_EPS = 64e-5
DECAY_OFFSET = 0.5
L2_EPS = 1e-12
VMEM_LIMIT = 48 * 1024 * 1024

_NN = (((1,), (0,)), ((), ()))
_NT = (((1,), (1,)), ((), ()))
_TN = (((0,), (0,)), ((), ()))


def _mm(a, b, dims=_NN):
    return lax.dot_general(a, b, dims, preferred_element_type=F32)


def _split2(x):
    hi = x.astype(BF16)
    lo = (x - hi.astype(F32)).astype(BF16)
    return hi, lo


def _split3(x):
    hi = x.astype(BF16)
    r = x - hi.astype(F32)
    mid = r.astype(BF16)
    lo = (r - mid.astype(F32)).astype(BF16)
    return hi, mid, lo


def _dot3(a, b, dims=_NN):
    ah, al = _split2(a)
    bh, bl = _split2(b)
    return _mm(ah, bh, dims) + (_mm(ah, bl, dims) + _mm(al, bh, dims))


def _softplus(z):
    neg_abs = pltpu.bitcast(pltpu.bitcast(z, jnp.uint32) | jnp.uint32(0x80000000), F32)
    return jnp.maximum(z, 0.0) + jnp.log(1.0 + jnp.exp(neg_abs))


def _silu(g):
    return g * jax.nn.sigmoid(g)


def _neg_strict_upper(n):
    ri = lax.broadcasted_iota(jnp.int32, (n, n), 0)
    ci = lax.broadcasted_iota(jnp.int32, (n, n), 1)
    return jnp.where(ri > ci, -1.0, 0.0).astype(BF16)


def _inproj_kernel(x_ref, gn_ref, w_ref, k_ref, v_ref, sg_ref, rw_ref, q_ref, kb_ref, vb_ref, *, sbw):
    x = x_ref[...]
    xn = x * lax.rsqrt(jnp.mean(x * x, axis=-1, keepdims=True) + RMS_EPS) * gn_ref[...]
    xb = xn.astype(BF16)

    def proj(lo, hi):
        return _mm(xb, w_ref[:, lo:hi])

    q = proj(0, sbw)
    q_ref[...] = (q * HEAD_DIM ** -0.5).astype(q_ref.dtype)
    k = proj(sbw, 2 * sbw)
    k_ref[...] = k
    kb_ref[...] = k.astype(BF16)
    v = proj(2 * sbw, 3 * sbw)
    v_ref[...] = v
    vb_ref[...] = v.astype(BF16)
    sg_ref[...] = _silu(proj(3 * sbw, 4 * sbw))
    rw_ref[...] = proj(4 * sbw, w_ref.shape[1])


def _inproj(x2d, gn, w_bf, sbw, q_dtype, tm):
    m, d = x2d.shape
    tm = min(tm, m)
    ncols = w_bf.shape[1]
    rwc = ncols - 4 * sbw
    row = lambda i: (i, 0)
    fixed = lambda i: (0, 0)
    slab = lambda dt: jax.ShapeDtypeStruct((m, sbw), dt)
    return pl.pallas_call(
        functools.partial(_inproj_kernel, sbw=sbw),
        grid=(m // tm,),
        in_specs=[pl.BlockSpec((tm, d), row), pl.BlockSpec((1, d), fixed), pl.BlockSpec((d, ncols), fixed)],
        out_specs=[pl.BlockSpec((tm, sbw), row), pl.BlockSpec((tm, sbw), row), pl.BlockSpec((tm, sbw), row),
                   pl.BlockSpec((tm, rwc), row), pl.BlockSpec((tm, sbw), row), pl.BlockSpec((tm, sbw), row),
                   pl.BlockSpec((tm, sbw), row)],
        out_shape=[slab(F32), slab(F32), slab(F32), jax.ShapeDtypeStruct((m, rwc), F32),
                   slab(q_dtype), slab(BF16), slab(BF16)],
        compiler_params=pltpu.CompilerParams(dimension_semantics=("arbitrary",), vmem_limit_bytes=VMEM_LIMIT),
        name="inproj",
    )(x2d, gn, w_bf)


def _sb_stages(zs, nu, mask):
    sps = [_softplus(z) for z in zs]
    if mask is not None:
        sps = [jnp.where(mask, sp, 0.0) for sp in sps]
    bts = [_mm(sp.astype(BF16), nu) for sp in sps]
    logw = [(z - sp) + bt for z, sp, bt in zip(zs, sps, bts)]
    tots = [bt[:, 0:1] - sp[:, 0:1] for sp, bt in zip(sps, bts)]
    return logw, tots


def _sb_weights(logw, carry, mask):
    a = jnp.exp(logw + carry)
    if mask is not None:
        a = jnp.where(mask, a, 0.0)
    return a


def _sb_prompt_kernel(bias_ref, q_ref, k_ref, v_ref, sg_ref, nu_ref, o_ref, *scr, t, npair):
    acc_scr, carry_scr = scr[:npair], scr[npair:]
    hg = pl.program_id(1)
    i = pl.program_id(2)
    in_h0 = lax.broadcasted_iota(jnp.int32, (t, LANES), 1) < HEAD_DIM
    nu = nu_ref[...]
    pairs = range(npair)
    qs, biases = [], []
    for p in pairs:
        q2 = q_ref[0, :, p * LANES:(p + 1) * LANES]
        zero = jnp.zeros_like(q2)
        qs.append(jnp.concatenate([jnp.where(in_h0, q2, zero), jnp.where(in_h0, zero, q2)], axis=0))
        head = (hg * npair + p) * 2
        biases.append((bias_ref[head], bias_ref[head + 1]))
        acc_scr[p][...] = jnp.zeros_like(acc_scr[p])
        carry_scr[p][...] = jnp.zeros_like(carry_scr[p])

    def tile(j, mask):
        start = pl.multiple_of(j * t, t)
        ks = [k_ref[0, pl.ds(start, t), p * LANES:(p + 1) * LANES] for p in pairs]
        vs = [v_ref[0, pl.ds(start, t), p * LANES:(p + 1) * LANES] for p in pairs]
        zs = [_mm(qs[p], ks[p], _NT) for p in pairs]
        zs = [jnp.concatenate([z[:t] + b[0], z[t:] + b[1]], axis=0) for z, b in zip(zs, biases)]
        logw, tots = _sb_stages(zs, nu, mask)
        ws = [_sb_weights(logw[p], carry_scr[p][...], mask) for p in pairs]
        pv = [_mm(ws[p].astype(BF16), vs[p]) for p in pairs]
        for p in pairs:
            acc_scr[p][...] += pv[p]
            carry_scr[p][...] += tots[p]

    qrow = lax.broadcasted_iota(jnp.int32, (2 * t, t), 0) % t
    kcol = lax.broadcasted_iota(jnp.int32, (2 * t, t), 1)
    tile(i, kcol < qrow)

    def body(s, _):
        tile(i - 1 - s, None)
        return 0

    lax.fori_loop(0, i, body, 0)
    outs = []
    for p in pairs:
        acc = acc_scr[p][...]
        outs.append(jnp.where(in_h0, acc[:t], acc[t:]))
    o = jnp.concatenate(outs, axis=1) * sg_ref[0]
    o_ref[0] = o.astype(o_ref.dtype)


def _sb_prompt(qb, kb, vb, sg, bias, t, npair):
    b, s, w = qb.shape
    gw = npair * LANES
    qmap = lambda bi, hg, i: (bi, i, hg)
    kvmap = lambda bi, hg, i: (bi, 0, hg)
    return pl.pallas_call(
        functools.partial(_sb_prompt_kernel, t=t, npair=npair),
        grid=(b, w // gw, s // t),
        in_specs=[pl.BlockSpec(memory_space=pltpu.SMEM),
                  pl.BlockSpec((1, t, gw), qmap),
                  pl.BlockSpec((1, s, gw), kvmap),
                  pl.BlockSpec((1, s, gw), kvmap),
                  pl.BlockSpec((1, t, gw), qmap),
                  pl.BlockSpec((t, t), lambda bi, hg, i: (0, 0))],
        out_specs=pl.BlockSpec((1, t, gw), qmap),
        out_shape=jax.ShapeDtypeStruct((b, s, w), BF16),
        scratch_shapes=[pltpu.VMEM((2 * t, LANES), F32)] * npair + [pltpu.VMEM((2 * t, 1), F32)] * npair,
        compiler_params=pltpu.CompilerParams(dimension_semantics=("arbitrary",) * 3, vmem_limit_bytes=VMEM_LIMIT),
        name="sb_prompt",
    )(bias, qb, kb, vb, sg, _neg_strict_upper(t))


def _sb_sample_kernel(pt_ref, q_ref, kn_ref, vn_ref, sg_ref, brow_ref, nu_ref, *rest, n_pages, t, h, page):
    page_refs = rest[:2 * n_pages]
    o_ref = rest[2 * n_pages]
    del pt_ref
    ht = h * t
    w = h * HEAD_DIM
    q = q_ref[0]
    brow = brow_ref[...]
    nu = nu_ref[...]

    row = lax.broadcasted_iota(jnp.int32, (ht, w), 0)
    lane = lax.broadcasted_iota(jnp.int32, (ht, w), 1)
    own = (row // t) == (lane // HEAD_DIM)
    qbd = jnp.where(own, jnp.concatenate([q] * h, axis=0), 0.0).astype(BF16)
    pad = jnp.zeros((page - t, w), F32)
    kn = jnp.concatenate([kn_ref[0], pad], axis=0).astype(BF16)
    vn = jnp.concatenate([vn_ref[0], pad], axis=0).astype(BF16)
    key = lax.broadcasted_iota(jnp.int32, (ht, page), 1)
    tok = lax.broadcasted_iota(jnp.int32, (ht, page), 0) % t
    logw, tots = _sb_stages([_mm(qbd, kn, _NT) + brow], nu, key < tok)
    a_new = _sb_weights(logw[0], jnp.zeros((ht, 1), F32), key < tok)
    sel = jnp.where(own, _mm(a_new.astype(BF16), vn), 0.0)
    o_new = sel[0:t]
    for hh in range(1, h):
        o_new = o_new + sel[hh * t:(hh + 1) * t]

    q3 = jnp.stack([q[:, hh * HEAD_DIM:(hh + 1) * HEAD_DIM] for hh in range(h)], axis=0).astype(BF16)
    pages = range(n_pages)
    kps = [page_refs[2 * s][0].astype(BF16) for s in pages]
    zs = [jnp.einsum("htd,hdk->htk", q3, kp, preferred_element_type=F32).reshape(ht, page) + brow for kp in kps]
    logw, ptots = _sb_stages(zs, nu, None)
    carry = tots[0]
    ws = []
    for s in pages:
        ws.append(_sb_weights(logw[s], carry, None))
        carry = carry + ptots[s]
    pv = [jnp.einsum("htk,hdk->htd", ws[s].reshape(h, t, page).astype(BF16), page_refs[2 * s + 1][0].astype(BF16),
                     preferred_element_type=F32) for s in pages]
    acc = pv[0]
    for s in range(1, n_pages):
        acc = acc + pv[s]
    o_past = jnp.concatenate([acc[hh] for hh in range(h)], axis=1)
    o_ref[0] = (o_new + o_past) * sg_ref[0]


def _sb_sample(q, kn, vn, sg, cache_kt, cache_vt, page_table, bias):
    bd, t, w = q.shape
    _, h, _, page = cache_kt.shape
    n_pages = page_table.shape[1]
    brow = jnp.broadcast_to(jnp.repeat(bias, t)[:, None], (h * t, page)).astype(F32)
    tok = lambda b, pt: (b, 0, 0)
    fixed = lambda b, pt: (0, 0)

    def page_map(s):
        return lambda b, pt: (pt[b, n_pages - 1 - s], 0, 0, 0)

    page_specs, page_args = [], []
    for s in range(n_pages):
        page_specs += [pl.BlockSpec((1, h, HEAD_DIM, page), page_map(s))] * 2
        page_args += [cache_kt, cache_vt]
    grid_spec = pltpu.PrefetchScalarGridSpec(
        num_scalar_prefetch=1,
        grid=(bd,),
        in_specs=[pl.BlockSpec((1, t, w), tok), pl.BlockSpec((1, t, w), tok), pl.BlockSpec((1, t, w), tok),
                  pl.BlockSpec((1, t, w), tok), pl.BlockSpec((h * t, page), fixed), pl.BlockSpec((page, page), fixed)]
        + page_specs,
        out_specs=pl.BlockSpec((1, t, w), tok),
    )
    return pl.pallas_call(
        functools.partial(_sb_sample_kernel, n_pages=n_pages, t=t, h=h, page=page),
        grid_spec=grid_spec,
        out_shape=jax.ShapeDtypeStruct((bd, t, w), F32),
        compiler_params=pltpu.CompilerParams(dimension_semantics=("arbitrary",), vmem_limit_bytes=VMEM_LIMIT),
        name="sb_sample",
    )(page_table, q, kn, vn, sg, brow, _neg_strict_upper(page), *page_args)


def _rwkv_kernel(p_ref, prev_ref, s0_ref, mu_ref, w0_ref, a0_ref, lora_ref, kk_ref, ka_ref, rk_ref,
                 lnw_ref, lnb_ref, o_ref, s_ref, prev_scr, *, g, tc, h):
    c = pl.program_id(1)
    n = g * tc
    w = h * HEAD_DIM
    hd = HEAD_DIM

    @pl.when(c == 0)
    def _():
        s_ref[...] = s0_ref[...]
        prev_scr[...] = prev_ref[...]

    p = p_ref[...]
    rowi = lax.broadcasted_iota(jnp.int32, (n, 1), 0)
    prev_full = jnp.concatenate([jnp.broadcast_to(prev_scr[gi], (tc, p.shape[1])) for gi in range(g)], axis=0)
    shifted = jnp.where(rowi % tc == 0, prev_full, pltpu.roll(p, 1, 0))
    for gi in range(g):
        prev_scr[gi] = p[(gi + 1) * tc - 1:(gi + 1) * tc, :]
    zc = p + (shifted - p) * mu_ref[...]

    r = zc[:, 0:w]
    k = zc[:, w:2 * w]
    v = zc[:, 2 * w:3 * w]
    gate = zc[:, 3 * w:4 * w]
    wa = zc[:, 4 * w:4 * w + 2 * LORA]
    lane = lax.broadcasted_iota(jnp.int32, wa.shape, 1)
    wa = jnp.where(lane < LORA, jnp.tanh(wa), wa)
    up = _dot3(wa, lora_ref[...])
    w_log = -_softplus(-(w0_ref[...] + up[:, 0:w])) - DECAY_OFFSET
    e = jnp.exp(w_log)
    alr = jax.nn.sigmoid(a0_ref[...] + up[:, w:2 * w])

    ri = lax.broadcasted_iota(jnp.int32, (n, 2 * n), 0)
    ci = lax.broadcasted_iota(jnp.int32, (n, 2 * n), 1) % n
    strict2 = ci < ri
    incl2 = ci <= ri
    if g > 1:
        same = (ri // tc) == (ci // tc)
        strict2 = strict2 & same
        incl2 = incl2 & same
    lmat = incl2[:, :n].astype(BF16)
    e_hi, e_mid, e_lo = _split3(e)
    cum = -(_mm(lmat, e_hi) + (_mm(lmat, e_mid) + _mm(lmat, e_lo)))
    p_in = jnp.exp(cum)
    p_ex = jnp.exp(cum + e)
    p_inv = jnp.exp(-cum)

    kkr = k * kk_ref[...]
    keff = k * (1.0 + (alr - 1.0) * ka_ref[...])
    at_u = -kkr * p_ex
    bt_u = kkr * alr * p_inv
    kt = keff * p_inv
    rt = r * p_in
    rkk = r * keff * rk_ref[...]
    lnw = lnw_ref[...]
    lnb = lnb_ref[...]

    heads = range(h)
    seqs = range(g)
    sls = [slice(hh * hd, (hh + 1) * hd) for hh in heads]
    kkr_h = [kkr[:, sl] for sl in sls]
    inv = [1.0 / jnp.maximum(jnp.sqrt(jnp.sum(x * x, axis=1, keepdims=True)), L2_EPS) for x in kkr_h]
    at_h = [at_u[:, sl] * iv for sl, iv in zip(sls, inv)]
    bt_h = [bt_u[:, sl] * iv for sl, iv in zip(sls, inv)]
    kt_h = [kt[:, sl] for sl in sls]
    rt_h = [rt[:, sl] for sl in sls]
    v_h = [v[:, sl] for sl in sls]
    bonus = [jnp.sum(rkk[:, sl], axis=1, keepdims=True) for sl in sls]
    s_old = [[s_ref[gi, hh] for hh in heads] for gi in seqs]
    bk = [jnp.concatenate([b_, k_], axis=0) for b_, k_ in zip(bt_h, kt_h)]
    x = [_dot3(jnp.concatenate([a_, r_], axis=0), bk_, _NT) for a_, r_, bk_ in zip(at_h, rt_h, bk)]
    a_both = [jnp.where(strict2, x_[:n], 0.0) for x_ in x]
    r_both = [jnp.where(incl2, x_[n:], 0.0) for x_ in x]
    zeros_v = jnp.zeros((n, hd), F32)
    akv = [_dot3(ab, jnp.concatenate([zeros_v, v_], axis=0)) for ab, v_ in zip(a_both, v_h)]
    pw = [ab[:, :n] for ab in a_both]
    sol = [jnp.concatenate([a_, k_], axis=1) for a_, k_ in zip(at_h, akv)]
    n_levels = tc.bit_length() - 1
    for lvl in range(n_levels):
        if lvl < n_levels - 1:
            res = [_dot3(p_, jnp.concatenate([s_, p_], axis=1)) for p_, s_ in zip(pw, sol)]
            sol = [s_ + r_[:, :2 * hd] for s_, r_ in zip(sol, res)]
            pw = [r_[:, 2 * hd:2 * hd + n] for r_ in res]
        else:
            sol = [s_ + _dot3(p_, s_) for p_, s_ in zip(pw, sol)]
    w_m = [s_[:, :hd] for s_ in sol]
    u0 = [s_[:, hd:2 * hd] for s_ in sol]
    rows = [slice(gi * tc, (gi + 1) * tc) for gi in seqs]
    xs = [[_dot3(jnp.concatenate([w_m[hh][rows[gi]], rt_h[hh][rows[gi]]], axis=0), s_old[gi][hh], _NT)
           for gi in seqs] for hh in heads]
    u_m, y_r = [], []
    for hh in heads:
        us = [xs[hh][gi][:tc] + u0[hh][rows[gi]] for gi in seqs]
        yr = [xs[hh][gi][tc:] for gi in seqs]
        u_m.append(us[0] if g == 1 else jnp.concatenate(us, axis=0))
        y_r.append(yr[0] if g == 1 else jnp.concatenate(yr, axis=0))
    uv = [jnp.concatenate([u_, v_], axis=0) for u_, v_ in zip(u_m, v_h)]
    y_h = [yr_ + _mm(rb.astype(BF16), uv_.astype(BF16)) for yr_, rb, uv_ in zip(y_r, r_both, uv)]
    row2 = jnp.concatenate([rowi, rowi], axis=0) // tc
    for gi in seqs:
        last = (gi + 1) * tc - 1
        uv_g = uv if g == 1 else [jnp.where(row2 == gi, uv_, 0.0) for uv_ in uv]
        upd = [_dot3(uv_g[hh], bk[hh], _TN) for hh in heads]
        for hh in heads:
            s_ref[gi, hh] = (s_old[gi][hh] + upd[hh]) * p_in[last:last + 1, sls[hh]]
    outs = []
    for hh in heads:
        mean = jnp.sum(y_h[hh], axis=1, keepdims=True) * (1.0 / hd)
        d = y_h[hh] - mean
        var = jnp.sum(d * d, axis=1, keepdims=True) * (1.0 / hd)
        outs.append(d * lax.rsqrt(var + GN_EPS) * lnw[:, sls[hh]] + lnb[:, sls[hh]] + bonus[hh] * v_h[hh])
    o_ref[...] = (jnp.concatenate(outs, axis=1) * _silu(gate)).astype(o_ref.dtype)


def _rwkv(p2d, prev, s0, params, g, tc, out_dtype):
    b, h = s0.shape[0], s0.shape[1]
    cols = p2d.shape[1]
    t = p2d.shape[0] // b
    nc = t // tc
    w = h * HEAD_DIM
    rows = g * tc
    fixed = lambda bi, c: (0, 0)
    seq = lambda bi, c: (bi, 0, 0)
    st = lambda bi, c: (bi, 0, 0, 0)
    rowmap = lambda bi, c: (bi * nc + c, 0)
    vec = lambda n: pl.BlockSpec((1, n), fixed)
    return pl.pallas_call(
        functools.partial(_rwkv_kernel, g=g, tc=tc, h=h),
        grid=(b // g, nc),
        in_specs=[pl.BlockSpec((rows, cols), rowmap), pl.BlockSpec((g, 1, cols), seq),
                  pl.BlockSpec((g, h, HEAD_DIM, HEAD_DIM), st),
                  vec(cols), vec(w), vec(w), pl.BlockSpec((2 * LORA, 2 * w), fixed),
                  vec(w), vec(w), vec(w), vec(w), vec(w)],
        out_specs=[pl.BlockSpec((rows, w), rowmap), pl.BlockSpec((g, h, HEAD_DIM, HEAD_DIM), st)],
        out_shape=[jax.ShapeDtypeStruct((b * t, w), out_dtype), jax.ShapeDtypeStruct(s0.shape, F32)],
        scratch_shapes=[pltpu.VMEM((g, 1, cols), F32)],
        compiler_params=pltpu.CompilerParams(dimension_semantics=("arbitrary",) * 2, vmem_limit_bytes=VMEM_LIMIT),
        name="rwkv",
    )(p2d, prev, s0, *params)


def _outproj_kernel(osb_ref, orw_ref, x_ref, w_ref, nf_ref, y_ref, *, sbw):
    hid = x_ref[...] + _mm(osb_ref[...].astype(BF16), w_ref[0:sbw, :]) + _mm(orw_ref[...].astype(BF16), w_ref[sbw:, :])
    y_ref[...] = hid * lax.rsqrt(jnp.mean(hid * hid, axis=-1, keepdims=True) + RMS_EPS) * nf_ref[...]


def _outproj(osb, orw, x2d, w_bf, nf, tm):
    m, d = x2d.shape
    tm = min(tm, m)
    sbw = osb.shape[1]
    row = lambda i: (i, 0)
    fixed = lambda i: (0, 0)
    return pl.pallas_call(
        functools.partial(_outproj_kernel, sbw=sbw),
        grid=(m // tm,),
        in_specs=[pl.BlockSpec((tm, sbw), row), pl.BlockSpec((tm, orw.shape[1]), row), pl.BlockSpec((tm, d), row),
                  pl.BlockSpec(w_bf.shape, fixed), pl.BlockSpec((1, d), fixed)],
        out_specs=pl.BlockSpec((tm, d), row),
        out_shape=jax.ShapeDtypeStruct((m, d), F32),
        compiler_params=pltpu.CompilerParams(dimension_semantics=("arbitrary",), vmem_limit_bytes=VMEM_LIMIT),
        name="outproj",
    )(osb, orw, x2d, w_bf, nf)


def _layer(h_p, h_s, cache_k, cache_v, page_table, s_wkv, s_shift,
           norm_in, w_in, sb_bias, mu, w0, w_up, a0, a_up, k_k, k_a, r_k, ln_w, ln_b, w_out, norm_out):
    b, s, d = h_p.shape
    bd, t, _ = h_s.shape
    n_heads = sb_bias.shape[0]
    sbw = n_heads * HEAD_DIM
    rw_heads = r_k.shape[0]
    rww = rw_heads * HEAD_DIM
    cols = mu.shape[0]

    w_in_bf = w_in.astype(BF16)
    w_out_bf = w_out.astype(BF16)
    gn = norm_in.reshape(1, d)
    nf = norm_out.reshape(1, d)
    zeros = jnp.zeros((LORA, rww), F32)
    lora = jnp.concatenate([jnp.concatenate([w_up, zeros], axis=1), jnp.concatenate([zeros, a_up], axis=1)], axis=0)
    rw_params = (mu.reshape(1, cols), w0.reshape(1, rww), a0.reshape(1, rww), lora, k_k.reshape(1, rww),
                 k_a.reshape(1, rww), r_k.reshape(1, rww), ln_w.reshape(1, rww), ln_b.reshape(1, rww))

    xp = h_p.reshape(b * s, d)
    k_p, v_p, sg_p, rw_p, qb_p, kb_p, vb_p = _inproj(xp, gn, w_in_bf, sbw, BF16, 256)
    three = lambda a: a.reshape(b, s, sbw)
    o_sb = _sb_prompt(three(qb_p), three(kb_p), three(vb_p), three(sg_p), sb_bias, min(256, s), 2)
    o_rw, wkv_p = _rwkv(rw_p, jnp.zeros((b, 1, cols), F32), jnp.zeros((b, rw_heads, HEAD_DIM, HEAD_DIM), F32),
                        rw_params, 1, 64, BF16)
    y_p = _outproj(o_sb.reshape(b * s, sbw), o_rw, xp, w_out_bf, nf, 512)
    shift_p = rw_p.reshape(b, s, cols)[:, s - 1]

    xs = h_s.reshape(bd * t, d)
    k_s, v_s, sg_s, rw_s, q_s, _, _ = _inproj(xs, gn, w_in_bf, sbw, F32, 256)
    three = lambda a: a.reshape(bd, t, sbw)
    os_sb = _sb_sample(three(q_s), three(k_s), three(v_s), three(sg_s), jnp.transpose(cache_k, (0, 2, 3, 1)),
                       jnp.transpose(cache_v, (0, 2, 3, 1)), page_table, sb_bias)
    os_rw, wkv_s = _rwkv(rw_s, s_shift.reshape(bd, 1, cols), s_wkv, rw_params, 8, t, F32)
    y_s = _outproj(os_sb.reshape(bd * t, sbw), os_rw, xs, w_out_bf, nf, 512)
    shift_s = rw_s.reshape(bd, t, cols)[:, t - 1]

    heads = lambda a, n, tt: a.reshape(n, tt, n_heads, HEAD_DIM)
    return (y_p.reshape(b, s, d), y_s.reshape(bd, t, d), heads(k_p, b, s), heads(v_p, b, s), wkv_p, shift_p,
            heads(k_s, bd, t), heads(v_s, bd, t), wkv_s, shift_s)


def kernel(x_prompt, x_sample, cache_k, cache_v, page_table, state_wkv, state_shift, norm_in, w_in, sb_bias,
           tshift_mu, w0, w_up, a0, a_up, k_k, k_a, r_k, ln_w, ln_b, w_out, norm_f):
    assert norm_in.shape[0] == 1, "single-layer trunk"
    out = _layer(x_prompt, x_sample, cache_k[0], cache_v[0], page_table, state_wkv[0], state_shift[0],
                 norm_in[0], w_in[0], sb_bias[0], tshift_mu[0], w0[0], w_up[0], a0[0], a_up[0],
                 k_k[0], k_a[0], r_k[0], ln_w[0], ln_b[0], w_out[0], norm_f)
    y_p, y_s = out[0], out[1]
    return (y_p, y_s) + tuple(o[None] for o in out[2:])
```

```python
import functools

import jax
import jax.numpy as jnp
from jax import lax
from jax.experimental import pallas as pl
from jax.experimental.pallas import tpu as pltpu

F32 = jnp.float32
BF16 = jnp.bfloat16

HEAD_DIM = 64
LANES = 128
LORA = 64
RMS_EPS = 1e-6
GN_EPS = 64e-5
DECAY_OFFSET = 0.5
L2_EPS = 1e-12
VMEM_LIMIT = 48 * 1024 * 1024

_NN = (((1,), (0,)), ((), ()))
_NT = (((1,), (1,)), ((), ()))
_TN = (((0,), (0,)), ((), ()))


def _mm(a, b, dims=_NN):
    return lax.dot_general(a, b, dims, preferred_element_type=F32)


def _split2(x):
    hi = x.astype(BF16)
    lo = (x - hi.astype(F32)).astype(BF16)
    return hi, lo


def _split3(x):
    hi = x.astype(BF16)
    r = x - hi.astype(F32)
    mid = r.astype(BF16)
    lo = (r - mid.astype(F32)).astype(BF16)
    return hi, mid, lo


def _dot1(a, b, dims=_NN):
    return _mm(a.astype(BF16), b.astype(BF16), dims)


def _dot3_tn(a, b):
    ah, al = _split2(a)
    bh, bl = _split2(b)
    return _mm(jnp.concatenate([ah, ah, al], axis=0), jnp.concatenate([bh, bl, bh], axis=0), _TN)


def _softplus(z):
    return jnp.maximum(z, 0.0) + jnp.log(1.0 + jnp.exp(-jnp.abs(z)))


def _silu(g):
    return g * jax.nn.sigmoid(g)


def _neg_strict_upper(n):
    ri = lax.broadcasted_iota(jnp.int32, (n, n), 0)
    ci = lax.broadcasted_iota(jnp.int32, (n, n), 1)
    return jnp.where(ri > ci, -1.0, 0.0).astype(BF16)


def _inproj_kernel(x_ref, gn_ref, w_ref, *rest, sbw, kv_major):
    if kv_major:
        wkv_ref, k_ref, v_ref, sg_ref, rw_ref, q_ref, kb_ref, vb_ref = rest
    else:
        k_ref, v_ref, sg_ref, rw_ref, q_ref, kb_ref, vb_ref = rest
    x = x_ref[...]
    xn = x * lax.rsqrt(jnp.mean(x * x, axis=-1, keepdims=True) + RMS_EPS) * gn_ref[...]
    xb = xn.astype(BF16)

    def proj(lo, hi):
        return _mm(xb, w_ref[:, lo:hi])

    q = proj(0, sbw)
    q_ref[...] = (q * HEAD_DIM ** -0.5).astype(q_ref.dtype)
    if kv_major:
        kv = _mm(wkv_ref[...], xb, _NT)
        k, v = kv[:sbw], kv[sbw:]
        k_ref[0], v_ref[0] = k, v
        kb_ref[0], vb_ref[0] = k.astype(BF16), v.astype(BF16)
    else:
        k = proj(sbw, 2 * sbw)
        v = proj(2 * sbw, 3 * sbw)
        k_ref[...], v_ref[...] = k, v
        kb_ref[...], vb_ref[...] = k.astype(BF16), v.astype(BF16)
    sg_ref[...] = _silu(proj(3 * sbw, 4 * sbw))
    rw_ref[...] = proj(4 * sbw, w_ref.shape[1])


def _inproj(x2d, gn, w_bf, sbw, q_dtype, tm, seqs=None):
    m, d = x2d.shape
    tm = min(tm, m)
    ncols = w_bf.shape[1]
    rwc = ncols - 4 * sbw
    row = lambda i: (i, 0)
    fixed = lambda i: (0, 0)
    slab = lambda dt: jax.ShapeDtypeStruct((m, sbw), dt)
    in_specs = [pl.BlockSpec((tm, d), row), pl.BlockSpec((1, d), fixed), pl.BlockSpec((d, ncols), fixed)]
    args = [x2d, gn, w_bf]
    if seqs is None:
        kv_spec = pl.BlockSpec((tm, sbw), row)
        kv_shape = slab
    else:
        per_seq = m // seqs // tm
        kv_spec = pl.BlockSpec((1, sbw, tm), lambda i: (i // per_seq, 0, i % per_seq))
        kv_shape = lambda dt: jax.ShapeDtypeStruct((seqs, sbw, m // seqs), dt)
        in_specs.append(pl.BlockSpec((2 * sbw, d), fixed))
        args.append(jnp.transpose(w_bf[:, sbw:3 * sbw]))
    return pl.pallas_call(
        functools.partial(_inproj_kernel, sbw=sbw, kv_major=seqs is not None),
        grid=(m // tm,),
        in_specs=in_specs,
        out_specs=[kv_spec, kv_spec, pl.BlockSpec((tm, sbw), row), pl.BlockSpec((tm, rwc), row),
                   pl.BlockSpec((tm, sbw), row), kv_spec, kv_spec],
        out_shape=[kv_shape(F32), kv_shape(F32), slab(F32), jax.ShapeDtypeStruct((m, rwc), F32),
                   slab(q_dtype), kv_shape(BF16), kv_shape(BF16)],
        compiler_params=pltpu.CompilerParams(dimension_semantics=("arbitrary",), vmem_limit_bytes=VMEM_LIMIT),
        name="inproj",
    )(*args)


def _sb_stages(zs, nu, mask):
    sps = [_softplus(z) for z in zs]
    if mask is not None:
        sps = [jnp.where(mask, sp, 0.0) for sp in sps]
    bts = [_mm(sp.astype(BF16), nu) for sp in sps]
    logw = [(z - sp) + bt for z, sp, bt in zip(zs, sps, bts)]
    tots = [bt[:, 0:1] - sp[:, 0:1] for sp, bt in zip(sps, bts)]
    return logw, tots


def _sb_weights(logw, carry, mask):
    a = jnp.exp(logw + carry)
    if mask is not None:
        a = jnp.where(mask, a, 0.0)
    return a


def _sb_prompt_kernel(bias_ref, q_ref, k_ref, v_ref, sg_ref, nu_ref, o_ref, *scr, t, npair):
    acc_scr, carry_scr = scr[:npair], scr[npair:]
    hg = pl.program_id(1)
    i = pl.program_id(2)
    in_h0 = lax.broadcasted_iota(jnp.int32, (t, LANES), 1) < HEAD_DIM
    nu = nu_ref[...]
    pairs = range(npair)
    qs, biases = [], []
    for p in pairs:
        q2 = q_ref[0, :, p * LANES:(p + 1) * LANES]
        zero = jnp.zeros_like(q2)
        qs.append(jnp.concatenate([jnp.where(in_h0, q2, zero), jnp.where(in_h0, zero, q2)], axis=0))
        head = (hg * npair + p) * 2
        biases.append((bias_ref[head], bias_ref[head + 1]))
        acc_scr[p][...] = jnp.zeros_like(acc_scr[p])
        carry_scr[p][...] = jnp.zeros_like(carry_scr[p])

    def tile(j, mask):
        start = pl.multiple_of(j * t, t)
        ks = [k_ref[0, p * LANES:(p + 1) * LANES, pl.ds(start, t)] for p in pairs]
        vs = [v_ref[0, p * LANES:(p + 1) * LANES, pl.ds(start, t)] for p in pairs]
        zs = [_mm(qs[p], ks[p]) for p in pairs]
        zs = [jnp.concatenate([z[:t] + b[0], z[t:] + b[1]], axis=0) for z, b in zip(zs, biases)]
        logw, tots = _sb_stages(zs, nu, mask)
        ws = [_sb_weights(logw[p], carry_scr[p][...], mask) for p in pairs]
        pv = [_mm(ws[p].astype(BF16), vs[p], _NT) for p in pairs]
        for p in pairs:
            acc_scr[p][...] += pv[p]
            carry_scr[p][...] += tots[p]

    qrow = lax.broadcasted_iota(jnp.int32, (2 * t, t), 0) % t
    kcol = lax.broadcasted_iota(jnp.int32, (2 * t, t), 1)
    tile(i, kcol < qrow)

    def body(s, _):
        tile(i - 1 - s, None)
        return 0

    lax.fori_loop(0, i, body, 0)
    outs = []
    for p in pairs:
        acc = acc_scr[p][...]
        outs.append(jnp.where(in_h0, acc[:t], acc[t:]))
    o = jnp.concatenate(outs, axis=1) * sg_ref[0]
    o_ref[0] = o.astype(o_ref.dtype)


def _sb_prompt(qb, kb, vb, sg, bias, t, npair):
    b, s, w = qb.shape
    gw = npair * LANES
    qmap = lambda bi, hg, i: (bi, i, hg)
    kvmap = lambda bi, hg, i: (bi, hg, 0)
    return pl.pallas_call(
        functools.partial(_sb_prompt_kernel, t=t, npair=npair),
        grid=(b, w // gw, s // t),
        in_specs=[pl.BlockSpec(memory_space=pltpu.SMEM),
                  pl.BlockSpec((1, t, gw), qmap),
                  pl.BlockSpec((1, gw, s), kvmap),
                  pl.BlockSpec((1, gw, s), kvmap),
                  pl.BlockSpec((1, t, gw), qmap),
                  pl.BlockSpec((t, t), lambda bi, hg, i: (0, 0))],
        out_specs=pl.BlockSpec((1, t, gw), qmap),
        out_shape=jax.ShapeDtypeStruct((b, s, w), BF16),
        scratch_shapes=[pltpu.VMEM((2 * t, LANES), F32)] * npair + [pltpu.VMEM((2 * t, 1), F32)] * npair,
        compiler_params=pltpu.CompilerParams(dimension_semantics=("arbitrary",) * 3, vmem_limit_bytes=VMEM_LIMIT),
        name="sb_prompt",
    )(bias, qb, kb, vb, sg, _neg_strict_upper(t))


def _sb_sample_kernel(pt_ref, q_ref, kn_ref, vn_ref, sg_ref, brow_ref, nu_ref, *rest, n_pages, t, h, page):
    page_refs = rest[:2 * n_pages]
    o_ref = rest[2 * n_pages]
    del pt_ref
    ht = h * t
    w = h * HEAD_DIM
    q = q_ref[0]
    brow = brow_ref[...]
    nu = nu_ref[...]

    row = lax.broadcasted_iota(jnp.int32, (ht, w), 0)
    lane = lax.broadcasted_iota(jnp.int32, (ht, w), 1)
    own = (row // t) == (lane // HEAD_DIM)
    qbd = jnp.where(own, jnp.concatenate([q] * h, axis=0), 0.0).astype(BF16)
    pad = jnp.zeros((page - t, w), F32)
    kn = jnp.concatenate([kn_ref[0], pad], axis=0).astype(BF16)
    vn = jnp.concatenate([vn_ref[0], pad], axis=0).astype(BF16)
    key = lax.broadcasted_iota(jnp.int32, (ht, page), 1)
    tok = lax.broadcasted_iota(jnp.int32, (ht, page), 0) % t
    new_mask = key < tok

    q3 = jnp.stack([q[:, hh * HEAD_DIM:(hh + 1) * HEAD_DIM] for hh in range(h)], axis=0).astype(BF16)
    pages = range(n_pages)
    z_new = _mm(qbd, kn, _NT) + brow
    kps = [page_refs[2 * s][0].astype(BF16) for s in pages]
    zs = [jnp.einsum("htd,hdk->htk", q3, kp, preferred_element_type=F32).reshape(ht, page) + brow for kp in kps]
    logw_new, tots_new = _sb_stages([z_new], nu, new_mask)
    logw, ptots = _sb_stages(zs, nu, None)
    a_new = _sb_weights(logw_new[0], jnp.zeros((ht, 1), F32), new_mask)
    carry = tots_new[0]
    ws = []
    for s in pages:
        ws.append(_sb_weights(logw[s], carry, None))
        carry = carry + ptots[s]
    pv_new = _mm(a_new.astype(BF16), vn)
    pv = [jnp.einsum("htk,hdk->htd", ws[s].reshape(h, t, page).astype(BF16), page_refs[2 * s + 1][0].astype(BF16),
                     preferred_element_type=F32) for s in pages]
    sel = jnp.where(own, pv_new, 0.0)
    o_new = sel[0:t]
    for hh in range(1, h):
        o_new = o_new + sel[hh * t:(hh + 1) * t]
    acc = pv[0]
    for s in range(1, n_pages):
        acc = acc + pv[s]
    o_past = jnp.concatenate([acc[hh] for hh in range(h)], axis=1)
    o_ref[0] = (o_new + o_past) * sg_ref[0]


def _sb_sample(q, kn, vn, sg, cache_kt, cache_vt, page_table, bias):
    bd, t, w = q.shape
    _, h, _, page = cache_kt.shape
    n_pages = page_table.shape[1]
    brow = jnp.broadcast_to(jnp.repeat(bias, t)[:, None], (h * t, page)).astype(F32)
    tok = lambda b, pt: (b, 0, 0)
    fixed = lambda b, pt: (0, 0)

    def page_map(s):
        return lambda b, pt: (pt[b, n_pages - 1 - s], 0, 0, 0)

    page_specs, page_args = [], []
    for s in range(n_pages):
        page_specs += [pl.BlockSpec((1, h, HEAD_DIM, page), page_map(s))] * 2
        page_args += [cache_kt, cache_vt]
    grid_spec = pltpu.PrefetchScalarGridSpec(
        num_scalar_prefetch=1,
        grid=(bd,),
        in_specs=[pl.BlockSpec((1, t, w), tok), pl.BlockSpec((1, t, w), tok), pl.BlockSpec((1, t, w), tok),
                  pl.BlockSpec((1, t, w), tok), pl.BlockSpec((h * t, page), fixed), pl.BlockSpec((page, page), fixed)]
        + page_specs,
        out_specs=pl.BlockSpec((1, t, w), tok),
    )
    return pl.pallas_call(
        functools.partial(_sb_sample_kernel, n_pages=n_pages, t=t, h=h, page=page),
        grid_spec=grid_spec,
        out_shape=jax.ShapeDtypeStruct((bd, t, w), F32),
        compiler_params=pltpu.CompilerParams(dimension_semantics=("arbitrary",), vmem_limit_bytes=VMEM_LIMIT),
        name="sb_sample",
    )(page_table, q, kn, vn, sg, brow, _neg_strict_upper(page), *page_args)


def _rwkv_kernel(p_ref, prev_ref, s0_ref, mu_ref, w0_ref, a0_ref, lora_ref, kk_ref, ka_ref, rk_ref,
                 lnw_ref, lnb_ref, o_ref, s_ref, prev_scr, *, g, tc, h, nsub):
    assert g == 1 or nsub == 1
    c = pl.program_id(1)
    n = g * tc
    span = nsub * tc
    nr = nsub * n
    w = h * HEAD_DIM
    hd = HEAD_DIM

    @pl.when(c == 0)
    def _():
        s_ref[...] = s0_ref[...]
        prev_scr[...] = prev_ref[...]

    p = p_ref[...]
    rowi = lax.broadcasted_iota(jnp.int32, (nr, 1), 0)
    prev_full = jnp.concatenate([jnp.broadcast_to(prev_scr[gi], (span, p.shape[1])) for gi in range(g)], axis=0)
    shifted = jnp.where(rowi % span == 0, prev_full, pltpu.roll(p, 1, 0))
    for gi in range(g):
        prev_scr[gi] = p[(gi + 1) * span - 1:(gi + 1) * span, :]
    zc = p + (shifted - p) * mu_ref[...]

    r = zc[:, 0:w]
    k = zc[:, w:2 * w]
    v = zc[:, 2 * w:3 * w]
    gate = zc[:, 3 * w:4 * w]
    wa = zc[:, 4 * w:4 * w + 2 * LORA]
    lane = lax.broadcasted_iota(jnp.int32, wa.shape, 1)
    wa = jnp.where(lane < LORA, jnp.tanh(wa), wa)
    up = _dot1(wa, lora_ref[...])
    w_log = -_softplus(-(w0_ref[...] + up[:, 0:w])) - DECAY_OFFSET
    e = jnp.exp(w_log)
    alr = jax.nn.sigmoid(a0_ref[...] + up[:, w:2 * w])

    ri = lax.broadcasted_iota(jnp.int32, (n, 2 * n), 0)
    ci = lax.broadcasted_iota(jnp.int32, (n, 2 * n), 1) % n
    strict2 = ci < ri
    incl2 = ci <= ri
    if g > 1:
        same = (ri // tc) == (ci // tc)
        strict2 = strict2 & same
        incl2 = incl2 & same
    li = lax.broadcasted_iota(jnp.int32, (nr, nr), 0)
    lj = lax.broadcasted_iota(jnp.int32, (nr, nr), 1)
    lmat = ((lj <= li) & ((li // tc) == (lj // tc))).astype(BF16)
    e_hi, e_mid, e_lo = _split3(e)
    cum = -(_mm(lmat, e_hi) + (_mm(lmat, e_mid) + _mm(lmat, e_lo)))
    p_in = jnp.exp(cum)
    p_ex = jnp.exp(cum + e)
    p_inv = jnp.exp(-cum)

    kkr = k * kk_ref[...]
    keff = k * (1.0 + (alr - 1.0) * ka_ref[...])
    at_u = -kkr * p_ex
    bt_u = kkr * alr * p_inv
    kt = keff * p_inv
    rt = r * p_in
    rkk = r * keff * rk_ref[...]
    lnw = lnw_ref[...]
    lnb = lnb_ref[...]

    heads = range(h)
    seqs = range(g)
    subs = range(nsub)
    sls = [slice(hh * hd, (hh + 1) * hd) for hh in heads]
    rs = [slice(sb * n, (sb + 1) * n) for sb in subs]
    chains = [(sb, hh) for sb in subs for hh in heads]
    cut = lambda arr: [arr[rs[sb], sls[hh]] for sb, hh in chains]
    inv = [1.0 / jnp.maximum(jnp.sqrt(jnp.sum(x * x, axis=1, keepdims=True)), L2_EPS) for x in cut(kkr)]
    at_c = [a_ * iv for a_, iv in zip(cut(at_u), inv)]
    bt_c = [b_ * iv for b_, iv in zip(cut(bt_u), inv)]
    kt_c, rt_c, v_c = cut(kt), cut(rt), cut(v)
    bonus = [jnp.sum(x, axis=1, keepdims=True) for x in cut(rkk)]
    bk = [jnp.concatenate([b_, k_], axis=0) for b_, k_ in zip(bt_c, kt_c)]
    x = [_dot1(jnp.concatenate([a_, r_], axis=0), bk_, _NT) for a_, r_, bk_ in zip(at_c, rt_c, bk)]
    a_both = [jnp.where(strict2, x_[:n], 0.0) for x_ in x]
    r_both = [jnp.where(incl2, x_[n:], 0.0) for x_ in x]
    zeros_v = jnp.zeros((n, hd), F32)
    akv = [_dot1(ab, jnp.concatenate([zeros_v, v_], axis=0)) for ab, v_ in zip(a_both, v_c)]
    pw = [ab[:, :n] for ab in a_both]
    sol = [jnp.concatenate([a_, k_], axis=1) for a_, k_ in zip(at_c, akv)]
    n_levels = tc.bit_length() - 1
    for lvl in range(n_levels):
        if lvl < n_levels - 1:
            res = [_dot1(p_, jnp.concatenate([s_, p_], axis=1)) for p_, s_ in zip(pw, sol)]
            sol = [s_ + r_[:, :2 * hd] for s_, r_ in zip(sol, res)]
            pw = [r_[:, 2 * hd:2 * hd + n] for r_ in res]
        else:
            sol = [s_ + _dot1(p_, s_) for p_, s_ in zip(pw, sol)]
    w_m = [s_[:, :hd] for s_ in sol]
    u0 = [s_[:, hd:2 * hd] for s_ in sol]

    state = [[s_ref[gi, hh] for hh in heads] for gi in seqs]
    rows = [slice(gi * tc, (gi + 1) * tc) for gi in seqs]
    row2 = jnp.concatenate([rowi[:n], rowi[:n]], axis=0) // tc
    out_rows = []
    for sb in subs:
        ch = [sb * h + hh for hh in heads]
        xs = [[_dot1(jnp.concatenate([w_m[c_][rows[gi]], rt_c[c_][rows[gi]]], axis=0), state[gi][hh], _NT)
               for gi in seqs] for hh, c_ in zip(heads, ch)]
        u_m, y_r = [], []
        for hh, c_ in zip(heads, ch):
            us = [xs[hh][gi][:tc] + u0[c_][rows[gi]] for gi in seqs]
            yr = [xs[hh][gi][tc:] for gi in seqs]
            u_m.append(us[0] if g == 1 else jnp.concatenate(us, axis=0))
            y_r.append(yr[0] if g == 1 else jnp.concatenate(yr, axis=0))
        uv = [jnp.concatenate([u_, v_c[c_]], axis=0) for u_, c_ in zip(u_m, ch)]
        y_h = [yr_ + _dot1(r_both[c_], uv_) for yr_, c_, uv_ in zip(y_r, ch, uv)]
        for gi in seqs:
            last = sb * n + (gi + 1) * tc - 1
            uv_g = uv if g == 1 else [jnp.where(row2 == gi, uv_, 0.0) for uv_ in uv]
            upd = [_dot3_tn(uv_g[hh], bk[ch[hh]]) for hh in heads]
            for hh in heads:
                state[gi][hh] = (state[gi][hh] + upd[hh]) * p_in[last:last + 1, sls[hh]]
        outs = []
        for hh, c_ in zip(heads, ch):
            mean = jnp.sum(y_h[hh], axis=1, keepdims=True) * (1.0 / hd)
            d = y_h[hh] - mean
            var = jnp.sum(d * d, axis=1, keepdims=True) * (1.0 / hd)
            outs.append(d * lax.rsqrt(var + GN_EPS) * lnw[:, sls[hh]] + lnb[:, sls[hh]] + bonus[c_] * v_c[c_])
        out_rows.append(jnp.concatenate(outs, axis=1))
    for gi in seqs:
        for hh in heads:
            s_ref[gi, hh] = state[gi][hh]
    y_all = out_rows[0] if nsub == 1 else jnp.concatenate(out_rows, axis=0)
    o_ref[...] = (y_all * _silu(gate)).astype(o_ref.dtype)


def _rwkv(p2d, prev, s0, params, g, tc, nsub, out_dtype):
    b, h = s0.shape[0], s0.shape[1]
    cols = p2d.shape[1]
    t = p2d.shape[0] // b
    nc = t // (tc * nsub)
    w = h * HEAD_DIM
    rows = g * tc * nsub
    fixed = lambda bi, c: (0, 0)
    seq = lambda bi, c: (bi, 0, 0)
    st = lambda bi, c: (bi, 0, 0, 0)
    rowmap = lambda bi, c: (bi * nc + c, 0)
    vec = lambda n: pl.BlockSpec((1, n), fixed)
    return pl.pallas_call(
        functools.partial(_rwkv_kernel, g=g, tc=tc, h=h, nsub=nsub),
        grid=(b // g, nc),
        in_specs=[pl.BlockSpec((rows, cols), rowmap), pl.BlockSpec((g, 1, cols), seq),
                  pl.BlockSpec((g, h, HEAD_DIM, HEAD_DIM), st),
                  vec(cols), vec(w), vec(w), pl.BlockSpec((2 * LORA, 2 * w), fixed),
                  vec(w), vec(w), vec(w), vec(w), vec(w)],
        out_specs=[pl.BlockSpec((rows, w), rowmap), pl.BlockSpec((g, h, HEAD_DIM, HEAD_DIM), st)],
        out_shape=[jax.ShapeDtypeStruct((b * t, w), out_dtype), jax.ShapeDtypeStruct(s0.shape, F32)],
        scratch_shapes=[pltpu.VMEM((g, 1, cols), F32)],
        compiler_params=pltpu.CompilerParams(dimension_semantics=("arbitrary",) * 2, vmem_limit_bytes=VMEM_LIMIT),
        name="rwkv",
    )(p2d, prev, s0, *params)


def _outproj_kernel(osb_ref, orw_ref, x_ref, w_ref, nf_ref, y_ref, *, sbw):
    hid = x_ref[...] + _mm(osb_ref[...].astype(BF16), w_ref[0:sbw, :]) + _mm(orw_ref[...].astype(BF16), w_ref[sbw:, :])
    y_ref[...] = hid * lax.rsqrt(jnp.mean(hid * hid, axis=-1, keepdims=True) + RMS_EPS) * nf_ref[...]


def _outproj(osb, orw, x2d, w_bf, nf, tm):
    m, d = x2d.shape
    tm = min(tm, m)
    sbw = osb.shape[1]
    row = lambda i: (i, 0)
    fixed = lambda i: (0, 0)
    return pl.pallas_call(
        functools.partial(_outproj_kernel, sbw=sbw),
        grid=(m // tm,),
        in_specs=[pl.BlockSpec((tm, sbw), row), pl.BlockSpec((tm, orw.shape[1]), row), pl.BlockSpec((tm, d), row),
                  pl.BlockSpec(w_bf.shape, fixed), pl.BlockSpec((1, d), fixed)],
        out_specs=pl.BlockSpec((tm, d), row),
        out_shape=jax.ShapeDtypeStruct((m, d), F32),
        compiler_params=pltpu.CompilerParams(dimension_semantics=("arbitrary",), vmem_limit_bytes=VMEM_LIMIT),
        name="outproj",
    )(osb, orw, x2d, w_bf, nf)


def _layer(h_p, h_s, cache_k, cache_v, page_table, s_wkv, s_shift,
           norm_in, w_in, sb_bias, mu, w0, w_up, a0, a_up, k_k, k_a, r_k, ln_w, ln_b, w_out, norm_out):
    b, s, d = h_p.shape
    bd, t, _ = h_s.shape
    n_heads = sb_bias.shape[0]
    sbw = n_heads * HEAD_DIM
    rw_heads = r_k.shape[0]
    rww = rw_heads * HEAD_DIM
    cols = mu.shape[0]

    w_in_bf = w_in.astype(BF16)
    w_out_bf = w_out.astype(BF16)
    gn = norm_in.reshape(1, d)
    nf = norm_out.reshape(1, d)
    zeros = jnp.zeros((LORA, rww), F32)
    lora = jnp.concatenate([jnp.concatenate([w_up, zeros], axis=1), jnp.concatenate([zeros, a_up], axis=1)], axis=0)
    rw_params = (mu.reshape(1, cols), w0.reshape(1, rww), a0.reshape(1, rww), lora, k_k.reshape(1, rww),
                 k_a.reshape(1, rww), r_k.reshape(1, rww), ln_w.reshape(1, rww), ln_b.reshape(1, rww))

    xp = h_p.reshape(b * s, d)
    k_p, v_p, sg_p, rw_p, qb_p, kb_p, vb_p = _inproj(xp, gn, w_in_bf, sbw, BF16, 256, seqs=b)
    three = lambda a: a.reshape(b, s, sbw)
    o_sb = _sb_prompt(three(qb_p), kb_p, vb_p, three(sg_p), sb_bias, min(256, s), 4)
    o_rw, wkv_p = _rwkv(rw_p, jnp.zeros((b, 1, cols), F32), jnp.zeros((b, rw_heads, HEAD_DIM, HEAD_DIM), F32),
                        rw_params, 1, 64, 2, BF16)
    y_p = _outproj(o_sb.reshape(b * s, sbw), o_rw, xp, w_out_bf, nf, 512)
    shift_p = rw_p.reshape(b, s, cols)[:, s - 1]

    xs = h_s.reshape(bd * t, d)
    k_s, v_s, sg_s, rw_s, q_s, _, _ = _inproj(xs, gn, w_in_bf, sbw, F32, 256)
    three = lambda a: a.reshape(bd, t, sbw)
    os_sb = _sb_sample(three(q_s), three(k_s), three(v_s), three(sg_s), jnp.transpose(cache_k, (0, 2, 3, 1)),
                       jnp.transpose(cache_v, (0, 2, 3, 1)), page_table, sb_bias)
    os_rw, wkv_s = _rwkv(rw_s, s_shift.reshape(bd, 1, cols), s_wkv, rw_params, 8, t, 1, F32)
    y_s = _outproj(os_sb.reshape(bd * t, sbw), os_rw, xs, w_out_bf, nf, 512)
    shift_s = rw_s.reshape(bd, t, cols)[:, t - 1]

    heads = lambda a, n, tt: a.reshape(n, tt, n_heads, HEAD_DIM)
    heads_t = lambda a: jnp.transpose(a.reshape(b, n_heads, HEAD_DIM, s), (0, 3, 1, 2))
    return (y_p.reshape(b, s, d), y_s.reshape(bd, t, d), heads_t(k_p), heads_t(v_p), wkv_p, shift_p,
            heads(k_s, bd, t), heads(v_s, bd, t), wkv_s, shift_s)


def kernel(x_prompt, x_sample, cache_k, cache_v, page_table, state_wkv, state_shift, norm_in, w_in, sb_bias,
           tshift_mu, w0, w_up, a0, a_up, k_k, k_a, r_k, ln_w, ln_b, w_out, norm_f):
    assert norm_in.shape[0] == 1, "single-layer trunk"
    out = _layer(x_prompt, x_sample, cache_k[0], cache_v[0], page_table, state_wkv[0], state_shift[0],
                 norm_in[0], w_in[0], sb_bias[0], tshift_mu[0], w0[0], w_up[0], a0[0], a_up[0],
                 k_k[0], k_a[0], r_k[0], ln_w[0], ln_b[0], w_out[0], norm_f)
    y_p, y_s = out[0], out[1]
    return (y_p, y_s) + tuple(o[None] for o in out[2:])
```

```python
import functools

import jax
import jax.numpy as jnp
from jax import lax
from jax.experimental import pallas as pl
from jax.experimental.pallas import tpu as pltpu

F32 = jnp.float32
BF16 = jnp.bfloat16

HEAD_DIM = 64
LANES = 128
LORA = 64
RMS_EPS = 1e-6
GN_EPS = 64e-5
DECAY_OFFSET = 0.5
L2_EPS = 1e-12
LOG2E = 1.4426950408889634
VMEM_LIMIT = 48 * 1024 * 1024

_NN = (((1,), (0,)), ((), ()))
_NT = (((1,), (1,)), ((), ()))
_TN = (((0,), (0,)), ((), ()))


def _mm(a, b, dims=_NN):
    return lax.dot_general(a, b, dims, preferred_element_type=F32)


def _split2(x):
    hi = x.astype(BF16)
    lo = (x - hi.astype(F32)).astype(BF16)
    return hi, lo


def _split3(x):
    hi = x.astype(BF16)
    r = x - hi.astype(F32)
    mid = r.astype(BF16)
    lo = (r - mid.astype(F32)).astype(BF16)
    return hi, mid, lo


def _dot1(a, b, dims=_NN):
    return _mm(a.astype(BF16), b.astype(BF16), dims)


def _dot3_tn(a, b):
    ah, al = _split2(a)
    bh, bl = _split2(b)
    return _mm(jnp.concatenate([ah, ah, al], axis=0), jnp.concatenate([bh, bl, bh], axis=0), _TN)


def _softplus(z):
    return jnp.maximum(z, 0.0) + jnp.log(1.0 + jnp.exp(-jnp.abs(z)))


def _softplus2(z):
    return jnp.maximum(z, 0.0) + jnp.log2(1.0 + jnp.exp2(-jnp.abs(z)))


def _silu(g):
    return g * jax.nn.sigmoid(g)


def _neg_strict_upper(n):
    ri = lax.broadcasted_iota(jnp.int32, (n, n), 0)
    ci = lax.broadcasted_iota(jnp.int32, (n, n), 1)
    return jnp.where(ri > ci, -1.0, 0.0).astype(BF16)


def _inproj_kernel(x_ref, gn_ref, w_ref, *rest, sbw, kv_major):
    if kv_major:
        wkv_ref, k_ref, v_ref, sg_ref, rw_ref, q_ref, kb_ref, vb_ref = rest
    else:
        k_ref, v_ref, sg_ref, rw_ref, q_ref, kb_ref, vb_ref = rest
    x = x_ref[...]
    xn = x * lax.rsqrt(jnp.mean(x * x, axis=-1, keepdims=True) + RMS_EPS) * gn_ref[...]
    xb = xn.astype(BF16)

    def proj(lo, hi):
        return _mm(xb, w_ref[:, lo:hi])

    q = proj(0, sbw)
    q_ref[...] = (q * (HEAD_DIM ** -0.5 * LOG2E)).astype(q_ref.dtype)
    if kv_major:
        kv = _mm(wkv_ref[...], xb, _NT)
        k, v = kv[:sbw], kv[sbw:]
        k_ref[0], v_ref[0] = k, v
        kb_ref[0], vb_ref[0] = k.astype(BF16), v.astype(BF16)
    else:
        k = proj(sbw, 2 * sbw)
        v = proj(2 * sbw, 3 * sbw)
        k_ref[...], v_ref[...] = k, v
        kb_ref[...], vb_ref[...] = k.astype(BF16), v.astype(BF16)
    sg_ref[...] = _silu(proj(3 * sbw, 4 * sbw))
    rw_ref[...] = proj(4 * sbw, w_ref.shape[1])


def _inproj(x2d, gn, w_bf, sbw, q_dtype, tm, seqs=None):
    m, d = x2d.shape
    tm = min(tm, m)
    ncols = w_bf.shape[1]
    rwc = ncols - 4 * sbw
    row = lambda i: (i, 0)
    fixed = lambda i: (0, 0)
    slab = lambda dt: jax.ShapeDtypeStruct((m, sbw), dt)
    in_specs = [pl.BlockSpec((tm, d), row), pl.BlockSpec((1, d), fixed), pl.BlockSpec((d, ncols), fixed)]
    args = [x2d, gn, w_bf]
    if seqs is None:
        kv_spec = pl.BlockSpec((tm, sbw), row)
        kv_shape = slab
    else:
        per_seq = m // seqs // tm
        kv_spec = pl.BlockSpec((1, sbw, tm), lambda i: (i // per_seq, 0, i % per_seq))
        kv_shape = lambda dt: jax.ShapeDtypeStruct((seqs, sbw, m // seqs), dt)
        in_specs.append(pl.BlockSpec((2 * sbw, d), fixed))
        args.append(jnp.transpose(w_bf[:, sbw:3 * sbw]))
    return pl.pallas_call(
        functools.partial(_inproj_kernel, sbw=sbw, kv_major=seqs is not None),
        grid=(m // tm,),
        in_specs=in_specs,
        out_specs=[kv_spec, kv_spec, pl.BlockSpec((tm, sbw), row), pl.BlockSpec((tm, rwc), row),
                   pl.BlockSpec((tm, sbw), row), kv_spec, kv_spec],
        out_shape=[kv_shape(F32), kv_shape(F32), slab(F32), jax.ShapeDtypeStruct((m, rwc), F32),
                   slab(q_dtype), kv_shape(BF16), kv_shape(BF16)],
        compiler_params=pltpu.CompilerParams(dimension_semantics=("arbitrary",), vmem_limit_bytes=VMEM_LIMIT),
        name="inproj",
    )(*args)


def _sb_stages(zs, nu, mask):
    sps = [_softplus2(z) for z in zs]
    if mask is not None:
        sps = [jnp.where(mask, sp, 0.0) for sp in sps]
    bts = [_mm(sp.astype(BF16), nu) for sp in sps]
    logw = [(z - sp) + bt for z, sp, bt in zip(zs, sps, bts)]
    tots = [bt[:, 0:1] - sp[:, 0:1] for sp, bt in zip(sps, bts)]
    return logw, tots


def _sb_weights(logw, carry, mask):
    a = jnp.exp2(logw + carry)
    if mask is not None:
        a = jnp.where(mask, a, 0.0)
    return a


def _sb_prompt_kernel(bias_ref, q_ref, k_ref, v_ref, sg_ref, nu_ref, o_ref, *scr, t, npair):
    acc_scr, carry_scr = scr[:npair], scr[npair:]
    hg = pl.program_id(1)
    i = pl.program_id(2)
    in_h0 = lax.broadcasted_iota(jnp.int32, (t, LANES), 1) < HEAD_DIM
    nu = nu_ref[...]
    pairs = range(npair)
    qs, biases = [], []
    for p in pairs:
        q2 = q_ref[0, :, p * LANES:(p + 1) * LANES]
        zero = jnp.zeros_like(q2)
        qs.append(jnp.concatenate([jnp.where(in_h0, q2, zero), jnp.where(in_h0, zero, q2)], axis=0))
        head = (hg * npair + p) * 2
        biases.append((bias_ref[head] * LOG2E, bias_ref[head + 1] * LOG2E))
        acc_scr[p][...] = jnp.zeros_like(acc_scr[p])
        carry_scr[p][...] = jnp.zeros_like(carry_scr[p])

    def tiles(js, mask):
        nb = len(js)
        starts = [pl.multiple_of(j * t, t) for j in js]
        ks = [k_ref[0, p * LANES:(p + 1) * LANES, pl.ds(st, t)] for st in starts for p in pairs]
        vs = [v_ref[0, p * LANES:(p + 1) * LANES, pl.ds(st, t)] for st in starts for p in pairs]
        zs = [_mm(qs[c % npair], ks[c]) for c in range(nb * npair)]
        zs = [jnp.concatenate([z[:t] + biases[c % npair][0], z[t:] + biases[c % npair][1]], axis=0)
              for c, z in enumerate(zs)]
        logw, tots = _sb_stages(zs, nu, mask)
        carries = [carry_scr[p][...] for p in pairs]
        ws = []
        for bi in range(nb):
            for p in pairs:
                ws.append(_sb_weights(logw[bi * npair + p], carries[p], mask))
                carries[p] = carries[p] + tots[bi * npair + p]
        pv = [_mm(ws[c].astype(BF16), vs[c], _NT) for c in range(nb * npair)]
        for p in pairs:
            tot = pv[p]
            for bi in range(1, nb):
                tot = tot + pv[bi * npair + p]
            acc_scr[p][...] += tot
            carry_scr[p][...] = carries[p]

    qrow = lax.broadcasted_iota(jnp.int32, (2 * t, t), 0) % t
    kcol = lax.broadcasted_iota(jnp.int32, (2 * t, t), 1)
    tiles([i], kcol < qrow)

    def body(s, _):
        j = i - 1 - 2 * s
        tiles([j, j - 1], None)
        return 0

    lax.fori_loop(0, i // 2, body, 0)

    @pl.when(i % 2 == 1)
    def _():
        tiles([0], None)

    outs = []
    for p in pairs:
        acc = acc_scr[p][...]
        outs.append(jnp.where(in_h0, acc[:t], acc[t:]))
    o = jnp.concatenate(outs, axis=1) * sg_ref[0]
    o_ref[0] = o.astype(o_ref.dtype)


def _sb_prompt(qb, kb, vb, sg, bias, t, npair):
    b, s, w = qb.shape
    gw = npair * LANES
    qmap = lambda bi, hg, i: (bi, i, hg)
    kvmap = lambda bi, hg, i: (bi, hg, 0)
    return pl.pallas_call(
        functools.partial(_sb_prompt_kernel, t=t, npair=npair),
        grid=(b, w // gw, s // t),
        in_specs=[pl.BlockSpec(memory_space=pltpu.SMEM),
                  pl.BlockSpec((1, t, gw), qmap),
                  pl.BlockSpec((1, gw, s), kvmap),
                  pl.BlockSpec((1, gw, s), kvmap),
                  pl.BlockSpec((1, t, gw), qmap),
                  pl.BlockSpec((t, t), lambda bi, hg, i: (0, 0))],
        out_specs=pl.BlockSpec((1, t, gw), qmap),
        out_shape=jax.ShapeDtypeStruct((b, s, w), BF16),
        scratch_shapes=[pltpu.VMEM((2 * t, LANES), F32)] * npair + [pltpu.VMEM((2 * t, 1), F32)] * npair,
        compiler_params=pltpu.CompilerParams(dimension_semantics=("arbitrary",) * 3, vmem_limit_bytes=VMEM_LIMIT),
        name="sb_prompt",
    )(bias, qb, kb, vb, sg, _neg_strict_upper(t))


def _sb_sample_kernel(pt_ref, q_ref, kn_ref, vn_ref, sg_ref, brow_ref, nu_ref, *rest, n_pages, t, h, page):
    page_refs = rest[:2 * n_pages]
    o_ref = rest[2 * n_pages]
    del pt_ref
    ht = h * t
    w = h * HEAD_DIM
    q = q_ref[0]
    brow = brow_ref[...]
    nu = nu_ref[...]

    row = lax.broadcasted_iota(jnp.int32, (ht, w), 0)
    lane = lax.broadcasted_iota(jnp.int32, (ht, w), 1)
    own = (row // t) == (lane // HEAD_DIM)
    qbd = jnp.where(own, jnp.concatenate([q] * h, axis=0), 0.0).astype(BF16)
    pad = jnp.zeros((page - t, w), F32)
    kn = jnp.concatenate([kn_ref[0], pad], axis=0).astype(BF16)
    vn = jnp.concatenate([vn_ref[0], pad], axis=0).astype(BF16)
    key = lax.broadcasted_iota(jnp.int32, (ht, page), 1)
    tok = lax.broadcasted_iota(jnp.int32, (ht, page), 0) % t
    new_mask = key < tok

    q3 = jnp.stack([q[:, hh * HEAD_DIM:(hh + 1) * HEAD_DIM] for hh in range(h)], axis=0).astype(BF16)
    pages = range(n_pages)
    z_new = _mm(qbd, kn, _NT) + brow
    kps = [page_refs[2 * s][0].astype(BF16) for s in pages]
    zs = [jnp.einsum("htd,hdk->htk", q3, kp, preferred_element_type=F32).reshape(ht, page) + brow for kp in kps]
    logw_new, tots_new = _sb_stages([z_new], nu, new_mask)
    logw, ptots = _sb_stages(zs, nu, None)
    a_new = _sb_weights(logw_new[0], jnp.zeros((ht, 1), F32), new_mask)
    carry = tots_new[0]
    ws = []
    for s in pages:
        ws.append(_sb_weights(logw[s], carry, None))
        carry = carry + ptots[s]
    pv_new = _mm(a_new.astype(BF16), vn)
    pv = [jnp.einsum("htk,hdk->htd", ws[s].reshape(h, t, page).astype(BF16), page_refs[2 * s + 1][0].astype(BF16),
                     preferred_element_type=F32) for s in pages]
    sel = jnp.where(own, pv_new, 0.0)
    o_new = sel[0:t]
    for hh in range(1, h):
        o_new = o_new + sel[hh * t:(hh + 1) * t]
    acc = pv[0]
    for s in range(1, n_pages):
        acc = acc + pv[s]
    o_past = jnp.concatenate([acc[hh] for hh in range(h)], axis=1)
    o_ref[0] = (o_new + o_past) * sg_ref[0]


def _sb_sample(q, kn, vn, sg, cache_kt, cache_vt, page_table, bias):
    bd, t, w = q.shape
    _, h, _, page = cache_kt.shape
    n_pages = page_table.shape[1]
    brow = jnp.broadcast_to(jnp.repeat(bias * LOG2E, t)[:, None], (h * t, page)).astype(F32)
    tok = lambda b, pt: (b, 0, 0)
    fixed = lambda b, pt: (0, 0)

    def page_map(s):
        return lambda b, pt: (pt[b, n_pages - 1 - s], 0, 0, 0)

    page_specs, page_args = [], []
    for s in range(n_pages):
        page_specs += [pl.BlockSpec((1, h, HEAD_DIM, page), page_map(s))] * 2
        page_args += [cache_kt, cache_vt]
    grid_spec = pltpu.PrefetchScalarGridSpec(
        num_scalar_prefetch=1,
        grid=(bd,),
        in_specs=[pl.BlockSpec((1, t, w), tok), pl.BlockSpec((1, t, w), tok), pl.BlockSpec((1, t, w), tok),
                  pl.BlockSpec((1, t, w), tok), pl.BlockSpec((h * t, page), fixed), pl.BlockSpec((page, page), fixed)]
        + page_specs,
        out_specs=pl.BlockSpec((1, t, w), tok),
    )
    return pl.pallas_call(
        functools.partial(_sb_sample_kernel, n_pages=n_pages, t=t, h=h, page=page),
        grid_spec=grid_spec,
        out_shape=jax.ShapeDtypeStruct((bd, t, w), F32),
        compiler_params=pltpu.CompilerParams(dimension_semantics=("arbitrary",), vmem_limit_bytes=VMEM_LIMIT),
        name="sb_sample",
    )(page_table, q, kn, vn, sg, brow, _neg_strict_upper(page), *page_args)


def _rwkv_kernel(p_ref, prev_ref, s0_ref, mu_ref, w0_ref, a0_ref, lora_ref, kk_ref, ka_ref, rk_ref,
                 lnw_ref, lnb_ref, o_ref, s_ref, prev_scr, *, g, tc, h, nsub):
    assert g == 1 or nsub == 1
    c = pl.program_id(1)
    n = g * tc
    span = nsub * tc
    nr = nsub * n
    w = h * HEAD_DIM
    hd = HEAD_DIM

    @pl.when(c == 0)
    def _():
        s_ref[...] = s0_ref[...]
        prev_scr[...] = prev_ref[...]

    p = p_ref[...]
    rowi = lax.broadcasted_iota(jnp.int32, (nr, 1), 0)
    prev_full = jnp.concatenate([jnp.broadcast_to(prev_scr[gi], (span, p.shape[1])) for gi in range(g)], axis=0)
    shifted = jnp.where(rowi % span == 0, prev_full, pltpu.roll(p, 1, 0))
    for gi in range(g):
        prev_scr[gi] = p[(gi + 1) * span - 1:(gi + 1) * span, :]
    zc = p + (shifted - p) * mu_ref[...]

    r = zc[:, 0:w]
    k = zc[:, w:2 * w]
    v = zc[:, 2 * w:3 * w]
    gate = zc[:, 3 * w:4 * w]
    wa = zc[:, 4 * w:4 * w + 2 * LORA]
    lane = lax.broadcasted_iota(jnp.int32, wa.shape, 1)
    wa = jnp.where(lane < LORA, jnp.tanh(wa), wa)
    up = _dot1(wa, lora_ref[...])
    w_log = -_softplus(-(w0_ref[...] + up[:, 0:w])) - DECAY_OFFSET
    e = jnp.exp(w_log)
    alr = jax.nn.sigmoid(a0_ref[...] + up[:, w:2 * w])

    ri = lax.broadcasted_iota(jnp.int32, (n, 2 * n), 0)
    ci = lax.broadcasted_iota(jnp.int32, (n, 2 * n), 1) % n
    strict2 = ci < ri
    incl2 = ci <= ri
    if g > 1:
        same = (ri // tc) == (ci // tc)
        strict2 = strict2 & same
        incl2 = incl2 & same
    li = lax.broadcasted_iota(jnp.int32, (nr, nr), 0)
    lj = lax.broadcasted_iota(jnp.int32, (nr, nr), 1)
    lmat = ((lj <= li) & ((li // tc) == (lj // tc))).astype(BF16)
    e_hi, e_mid, e_lo = _split3(e)
    cum = -(_mm(lmat, e_hi) + (_mm(lmat, e_mid) + _mm(lmat, e_lo)))
    p_in = jnp.exp(cum)
    p_ex = jnp.exp(cum + e)
    p_inv = jnp.exp(-cum)

    kkr = k * kk_ref[...]
    keff = k * (1.0 + (alr - 1.0) * ka_ref[...])
    at_u = -kkr * p_ex
    bt_u = kkr * alr * p_inv
    kt = keff * p_inv
    rt = r * p_in
    rkk = r * keff * rk_ref[...]
    lnw = lnw_ref[...]
    lnb = lnb_ref[...]

    heads = range(h)
    seqs = range(g)
    subs = range(nsub)
    sls = [slice(hh * hd, (hh + 1) * hd) for hh in heads]
    rs = [slice(sb * n, (sb + 1) * n) for sb in subs]
    chains = [(sb, hh) for sb in subs for hh in heads]
    cut = lambda arr: [arr[rs[sb], sls[hh]] for sb, hh in chains]
    inv = [1.0 / jnp.maximum(jnp.sqrt(jnp.sum(x * x, axis=1, keepdims=True)), L2_EPS) for x in cut(kkr)]
    at_c = [a_ * iv for a_, iv in zip(cut(at_u), inv)]
    bt_c = [b_ * iv for b_, iv in zip(cut(bt_u), inv)]
    kt_c, rt_c, v_c = cut(kt), cut(rt), cut(v)
    bonus = [jnp.sum(x, axis=1, keepdims=True) for x in cut(rkk)]
    bk = [jnp.concatenate([b_, k_], axis=0) for b_, k_ in zip(bt_c, kt_c)]
    x = [_dot1(jnp.concatenate([a_, r_], axis=0), bk_, _NT) for a_, r_, bk_ in zip(at_c, rt_c, bk)]
    a_both = [jnp.where(strict2, x_[:n], 0.0) for x_ in x]
    r_both = [jnp.where(incl2, x_[n:], 0.0) for x_ in x]
    zeros_v = jnp.zeros((n, hd), F32)
    akv = [_dot1(ab, jnp.concatenate([zeros_v, v_], axis=0)) for ab, v_ in zip(a_both, v_c)]
    pw = [ab[:, :n] for ab in a_both]
    sol = [jnp.concatenate([a_, k_], axis=1) for a_, k_ in zip(at_c, akv)]
    n_levels = tc.bit_length() - 1
    for lvl in range(n_levels):
        if lvl < n_levels - 1:
            res = [_dot1(p_, jnp.concatenate([s_, p_], axis=1)) for p_, s_ in zip(pw, sol)]
            sol = [s_ + r_[:, :2 * hd] for s_, r_ in zip(sol, res)]
            pw = [r_[:, 2 * hd:2 * hd + n] for r_ in res]
        else:
            sol = [s_ + _dot1(p_, s_) for p_, s_ in zip(pw, sol)]
    w_m = [s_[:, :hd] for s_ in sol]
    u0 = [s_[:, hd:2 * hd] for s_ in sol]

    state = [[s_ref[gi, hh] for hh in heads] for gi in seqs]
    rows = [slice(gi * tc, (gi + 1) * tc) for gi in seqs]
    row2 = jnp.concatenate([rowi[:n], rowi[:n]], axis=0) // tc
    out_rows = []
    for sb in subs:
        ch = [sb * h + hh for hh in heads]
        xs = [[_dot1(jnp.concatenate([w_m[c_][rows[gi]], rt_c[c_][rows[gi]]], axis=0), state[gi][hh], _NT)
               for gi in seqs] for hh, c_ in zip(heads, ch)]
        u_m, y_r = [], []
        for hh, c_ in zip(heads, ch):
            us = [xs[hh][gi][:tc] + u0[c_][rows[gi]] for gi in seqs]
            yr = [xs[hh][gi][tc:] for gi in seqs]
            u_m.append(us[0] if g == 1 else jnp.concatenate(us, axis=0))
            y_r.append(yr[0] if g == 1 else jnp.concatenate(yr, axis=0))
        uv = [jnp.concatenate([u_, v_c[c_]], axis=0) for u_, c_ in zip(u_m, ch)]
        y_h = [yr_ + _dot1(r_both[c_], uv_) for yr_, c_, uv_ in zip(y_r, ch, uv)]
        for gi in seqs:
            last = sb * n + (gi + 1) * tc - 1
            uv_g = uv if g == 1 else [jnp.where(row2 == gi, uv_, 0.0) for uv_ in uv]
            upd = [_dot3_tn(uv_g[hh], bk[ch[hh]]) for hh in heads]
            for hh in heads:
                state[gi][hh] = (state[gi][hh] + upd[hh]) * p_in[last:last + 1, sls[hh]]
        outs = []
        for hh, c_ in zip(heads, ch):
            mean = jnp.sum(y_h[hh], axis=1, keepdims=True) * (1.0 / hd)
            d = y_h[hh] - mean
            var = jnp.sum(d * d, axis=1, keepdims=True) * (1.0 / hd)
            outs.append(d * lax.rsqrt(var + GN_EPS) * lnw[:, sls[hh]] + lnb[:, sls[hh]] + bonus[c_] * v_c[c_])
        out_rows.append(jnp.concatenate(outs, axis=1))
    for gi in seqs:
        for hh in heads:
            s_ref[gi, hh] = state[gi][hh]
    y_all = out_rows[0] if nsub == 1 else jnp.concatenate(out_rows, axis=0)
    o_ref[...] = (y_all * _silu(gate)).astype(o_ref.dtype)


def _rwkv(p2d, prev, s0, params, g, tc, nsub, out_dtype):
    b, h = s0.shape[0], s0.shape[1]
    cols = p2d.shape[1]
    t = p2d.shape[0] // b
    nc = t // (tc * nsub)
    w = h * HEAD_DIM
    rows = g * tc * nsub
    fixed = lambda bi, c: (0, 0)
    seq = lambda bi, c: (bi, 0, 0)
    st = lambda bi, c: (bi, 0, 0, 0)
    rowmap = lambda bi, c: (bi * nc + c, 0)
    vec = lambda n: pl.BlockSpec((1, n), fixed)
    return pl.pallas_call(
        functools.partial(_rwkv_kernel, g=g, tc=tc, h=h, nsub=nsub),
        grid=(b // g, nc),
        in_specs=[pl.BlockSpec((rows, cols), rowmap), pl.BlockSpec((g, 1, cols), seq),
                  pl.BlockSpec((g, h, HEAD_DIM, HEAD_DIM), st),
                  vec(cols), vec(w), vec(w), pl.BlockSpec((2 * LORA, 2 * w), fixed),
                  vec(w), vec(w), vec(w), vec(w), vec(w)],
        out_specs=[pl.BlockSpec((rows, w), rowmap), pl.BlockSpec((g, h, HEAD_DIM, HEAD_DIM), st)],
        out_shape=[jax.ShapeDtypeStruct((b * t, w), out_dtype), jax.ShapeDtypeStruct(s0.shape, F32)],
        scratch_shapes=[pltpu.VMEM((g, 1, cols), F32)],
        compiler_params=pltpu.CompilerParams(dimension_semantics=("arbitrary",) * 2, vmem_limit_bytes=VMEM_LIMIT),
        name="rwkv",
    )(p2d, prev, s0, *params)


def _outproj_kernel(osb_ref, orw_ref, x_ref, w_ref, nf_ref, y_ref, *, sbw):
    hid = x_ref[...] + _mm(osb_ref[...].astype(BF16), w_ref[0:sbw, :]) + _mm(orw_ref[...].astype(BF16), w_ref[sbw:, :])
    y_ref[...] = hid * lax.rsqrt(jnp.mean(hid * hid, axis=-1, keepdims=True) + RMS_EPS) * nf_ref[...]


def _outproj(osb, orw, x2d, w_bf, nf, tm):
    m, d = x2d.shape
    tm = min(tm, m)
    sbw = osb.shape[1]
    row = lambda i: (i, 0)
    fixed = lambda i: (0, 0)
    return pl.pallas_call(
        functools.partial(_outproj_kernel, sbw=sbw),
        grid=(m // tm,),
        in_specs=[pl.BlockSpec((tm, sbw), row), pl.BlockSpec((tm, orw.shape[1]), row), pl.BlockSpec((tm, d), row),
                  pl.BlockSpec(w_bf.shape, fixed), pl.BlockSpec((1, d), fixed)],
        out_specs=pl.BlockSpec((tm, d), row),
        out_shape=jax.ShapeDtypeStruct((m, d), F32),
        compiler_params=pltpu.CompilerParams(dimension_semantics=("arbitrary",), vmem_limit_bytes=VMEM_LIMIT),
        name="outproj",
    )(osb, orw, x2d, w_bf, nf)


def _layer(h_p, h_s, cache_k, cache_v, page_table, s_wkv, s_shift,
           norm_in, w_in, sb_bias, mu, w0, w_up, a0, a_up, k_k, k_a, r_k, ln_w, ln_b, w_out, norm_out):
    b, s, d = h_p.shape
    bd, t, _ = h_s.shape
    n_heads = sb_bias.shape[0]
    sbw = n_heads * HEAD_DIM
    rw_heads = r_k.shape[0]
    rww = rw_heads * HEAD_DIM
    cols = mu.shape[0]

    w_in_bf = w_in.astype(BF16)
    w_out_bf = w_out.astype(BF16)
    gn = norm_in.reshape(1, d)
    nf = norm_out.reshape(1, d)
    zeros = jnp.zeros((LORA, rww), F32)
    lora = jnp.concatenate([jnp.concatenate([w_up, zeros], axis=1), jnp.concatenate([zeros, a_up], axis=1)], axis=0)
    rw_params = (mu.reshape(1, cols), w0.reshape(1, rww), a0.reshape(1, rww), lora, k_k.reshape(1, rww),
                 k_a.reshape(1, rww), r_k.reshape(1, rww), ln_w.reshape(1, rww), ln_b.reshape(1, rww))

    xp = h_p.reshape(b * s, d)
    k_p, v_p, sg_p, rw_p, qb_p, kb_p, vb_p = _inproj(xp, gn, w_in_bf, sbw, BF16, 256, seqs=b)
    three = lambda a: a.reshape(b, s, sbw)
    o_sb = _sb_prompt(three(qb_p), kb_p, vb_p, three(sg_p), sb_bias, min(256, s), 4)
    o_rw, wkv_p = _rwkv(rw_p, jnp.zeros((b, 1, cols), F32), jnp.zeros((b, rw_heads, HEAD_DIM, HEAD_DIM), F32),
                        rw_params, 1, 64, 2, BF16)
    y_p = _outproj(o_sb.reshape(b * s, sbw), o_rw, xp, w_out_bf, nf, 512)
    shift_p = rw_p.reshape(b, s, cols)[:, s - 1]

    xs = h_s.reshape(bd * t, d)
    k_s, v_s, sg_s, rw_s, q_s, _, _ = _inproj(xs, gn, w_in_bf, sbw, F32, 256)
    three = lambda a: a.reshape(bd, t, sbw)
    os_sb = _sb_sample(three(q_s), three(k_s), three(v_s), three(sg_s), jnp.transpose(cache_k, (0, 2, 3, 1)),
                       jnp.transpose(cache_v, (0, 2, 3, 1)), page_table, sb_bias)
    os_rw, wkv_s = _rwkv(rw_s, s_shift.reshape(bd, 1, cols), s_wkv, rw_params, 8, t, 1, F32)
    y_s = _outproj(os_sb.reshape(bd * t, sbw), os_rw, xs, w_out_bf, nf, 512)
    shift_s = rw_s.reshape(bd, t, cols)[:, t - 1]

    heads = lambda a, n, tt: a.reshape(n, tt, n_heads, HEAD_DIM)
    heads_t = lambda a: jnp.transpose(a.reshape(b, n_heads, HEAD_DIM, s), (0, 3, 1, 2))
    return (y_p.reshape(b, s, d), y_s.reshape(bd, t, d), heads_t(k_p), heads_t(v_p), wkv_p, shift_p,
            heads(k_s, bd, t), heads(v_s, bd, t), wkv_s, shift_s)


def kernel(x_prompt, x_sample, cache_k, cache_v, page_table, state_wkv, state_shift, norm_in, w_in, sb_bias,
           tshift_mu, w0, w_up, a0, a_up, k_k, k_a, r_k, ln_w, ln_b, w_out, norm_f):
    assert norm_in.shape[0] == 1, "single-layer trunk"
    out = _layer(x_prompt, x_sample, cache_k[0], cache_v[0], page_table, state_wkv[0], state_shift[0],
                 norm_in[0], w_in[0], sb_bias[0], tshift_mu[0], w0[0], w_up[0], a0[0], a_up[0],
                 k_k[0], k_a[0], r_k[0], ln_w[0], ln_b[0], w_out[0], norm_f)
    y_p, y_s = out[0], out[1]
    return (y_p, y_s) + tuple(o[None] for o in out[2:])
```
